```python
import math
import jax, jax.numpy as jnp
from jax import lax
import numpy as np

D_MODEL = 2048
BATCH = 32
SEQ = 256
DEPTH = 2
DEC_BATCH = 4
DEC_SEQ = 1024
PAST_LEN = 256

GRID_W = 64
D_BRANCH = 512
D_MIX = 2048
A_HEADS = 4
A_KV_HEADS = 2
A_HEAD_DIM = 128
A_WINDOW = 128
A_BLOCK = 128
ROPE_THETA = 10000.0
B_HEADS = 8
B_HEAD_DIM = 64
B_GROUPS = 2
B_STATE = 128
B_CONV = 5
B_CHUNK = 128
B_XBC = 1024
POOL_WINDOWS = (2, 4, 8, 16)
N_POOL = 4
POOL_GROUP = 128
D_HEADS = 4
D_HEAD_DIM = 128
NA_KH = 8
NA_KW = 16
PROJ_SIZES = (512, 256, 256, 512,
              1024, 512, 16,
              512, 512,
              512, 512, 512, 512)
D_PROJ = 6160
LN_EPS = 1e-6
NEG_INF = -1e30
F32 = jnp.float32

kernel_name = 'hybrid_diffusion_parallel_heads_step'


def _ln(x):
    xf = x.astype(F32)
    mu = jnp.mean(xf, -1, keepdims=True)
    var = jnp.mean(jnp.square(xf - mu), -1, keepdims=True)
    return ((xf - mu) * lax.rsqrt(var + LN_EPS)).astype(x.dtype)


def _rms(x, w):
    xf = x.astype(F32)
    y = xf * lax.rsqrt(jnp.mean(xf * xf, -1, keepdims=True) + LN_EPS)
    return y.astype(x.dtype) * w


def _split_proj(proj):
    idx = []
    acc = 0
    for s in PROJ_SIZES[:-1]:
        acc += s
        idx.append(acc)
    return jnp.split(proj, idx, axis=-1)


def _grid_pos(T):
    t = jnp.arange(T)
    return (t // GRID_W).astype(F32), (t % GRID_W).astype(F32)


def _rope_axial(x, rows, cols):
    hd = x.shape[-1]
    ax = hd // 2
    nf = ax // 2
    inv = ROPE_THETA ** (-jnp.arange(nf, dtype=F32) / nf)

    def rot(xa, pos):
        ang = pos[:, None] * inv[None, :]
        cos = jnp.cos(ang)[None, :, None, :]
        sin = jnp.sin(ang)[None, :, None, :]
        x1 = xa[..., :nf].astype(F32)
        x2 = xa[..., nf:].astype(F32)
        return jnp.concatenate([x1 * cos - x2 * sin, x1 * sin + x2 * cos], -1)

    return jnp.concatenate([rot(x[..., :ax], rows), rot(x[..., ax:], cols)], -1).astype(x.dtype)


def _ctx_attention(q, k, v, sink):
    Bn, L, H, hd = q.shape
    KV = k.shape[2]
    G = H // KV
    qg = q.reshape(Bn, L, KV, G, hd)
    s = jnp.einsum('blkgd,bmkd->bkglm', qg, k, preferred_element_type=F32) * (hd ** -0.5)
    if sink is not None:
        s_sink = jnp.broadcast_to(sink.astype(F32).reshape(1, KV, G, 1, 1), s.shape[:-1] + (1,))
        s = jnp.concatenate([s, s_sink], -1)
    p = jax.nn.softmax(s, -1)[..., :L].astype(v.dtype)
    o = jnp.einsum('bkglm,bmkd->blkgd', p, v)
    return o.reshape(Bn, L, H * hd)


def _window_attention(q, k, v, k_ctx, v_ctx, sink):
    Bn, T, H, hd = q.shape
    KV = k.shape[2]
    G = H // KV
    nb = T // A_BLOCK
    Lc = k_ctx.shape[1]
    scale = hd ** -0.5
    qb = q.reshape(Bn, nb, A_BLOCK, KV, G, hd)

    def bands(a):
        ap = jnp.pad(a, ((0, 0), (A_BLOCK, A_BLOCK), (0, 0), (0, 0)))
        ap = ap.reshape(Bn, nb + 2, A_BLOCK, KV, hd)
        return jnp.concatenate([ap[:, :-2], ap[:, 1:-1], ap[:, 2:]], axis=2)

    kw = bands(k)
    vw = bands(v)
    s_loc = jnp.einsum('bnqkgd,bnmkd->bnkgqm', qb, kw, preferred_element_type=F32) * scale
    qpos = jnp.arange(nb)[:, None] * A_BLOCK + jnp.arange(A_BLOCK)[None, :]
    kpos = jnp.arange(nb)[:, None] * A_BLOCK - A_BLOCK + jnp.arange(3 * A_BLOCK)[None, :]
    valid = ((jnp.abs(kpos[:, None, :] - qpos[:, :, None]) <= A_WINDOW)
             & (kpos >= 0)[:, None, :] & (kpos < T)[:, None, :])
    s_loc = jnp.where(valid[None, :, None, None], s_loc, NEG_INF)
    s_ctx = jnp.einsum('bnqkgd,blkd->bnkgql', qb, k_ctx, preferred_element_type=F32) * scale
    s_sink = jnp.broadcast_to(sink.astype(F32).reshape(1, 1, KV, G, 1, 1), s_loc.shape[:-1] + (1,))
    p = jax.nn.softmax(jnp.concatenate([s_loc, s_ctx, s_sink], -1), -1)
    nw = 3 * A_BLOCK
    p_loc = p[..., :nw].astype(v.dtype)
    p_ctx = p[..., nw:nw + Lc].astype(v.dtype)
    o = (jnp.einsum('bnkgqm,bnmkd->bnqkgd', p_loc, vw)
         + jnp.einsum('bnkgql,blkd->bnqkgd', p_ctx, v_ctx))
    return o.reshape(Bn, T, H * hd)


def _neighbourhood_attention(q, k, v, k_ctx, v_ctx, rpb):
    Bn, T, H, hd = q.shape
    rows = T // GRID_W
    kh = min(NA_KH, rows)
    Lc = k_ctx.shape[1]
    scale = hd ** -0.5
    r = jnp.arange(rows)
    rs = jnp.clip(r - kh // 2, 0, rows - kh)
    key_rows = rs[:, None] + jnp.arange(kh)[None, :]
    c = jnp.arange(GRID_W)
    cs = jnp.clip(c - NA_KW // 2, 0, GRID_W - NA_KW)
    qg = q.reshape(Bn, rows, GRID_W, H, hd)
    kg = k.reshape(Bn, rows, GRID_W, H, hd)[:, key_rows]
    vg = v.reshape(Bn, rows, GRID_W, H, hd)[:, key_rows]
    s_loc = jnp.einsum('brqhd,brjkhd->brhqjk', qg, kg, preferred_element_type=F32) * scale
    dy = key_rows - r[:, None]
    dx = jnp.clip(c[None, :] - c[:, None], -(NA_KW - 1), NA_KW - 1)
    col_ok = (c[None, :] >= cs[:, None]) & (c[None, :] < cs[:, None] + NA_KW)
    bias = rpb[:, (dy + NA_KH - 1)[:, :, None, None], (dx + NA_KW - 1)[None, None, :, :]]
    bias = bias.transpose(1, 0, 3, 2, 4).astype(F32)
    s_loc = jnp.where(col_ok[:, None, :], s_loc + bias[None], NEG_INF)
    nl = kh * GRID_W
    s_loc = s_loc.reshape(Bn, rows, H, GRID_W, nl)
    s_ctx = jnp.einsum('brqhd,blhd->brhql', qg, k_ctx, preferred_element_type=F32) * scale
    p = jax.nn.softmax(jnp.concatenate([s_loc, s_ctx], -1), -1)
    p_loc = p[..., :nl].reshape(Bn, rows, H, GRID_W, kh, GRID_W).astype(v.dtype)
    p_ctx = p[..., nl:nl + Lc].astype(v.dtype)
    o = (jnp.einsum('brhqjk,brjkhd->brqhd', p_loc, vg)
         + jnp.einsum('brhql,blhd->brqhd', p_ctx, v_ctx))
    return o.reshape(Bn, T, H * hd)


def _pool_branch(p, w_pool, b_pool, pool_scale):
    Bn, T, C = p.shape
    cs = jnp.pad(jnp.cumsum(p.astype(F32), axis=1), ((0, 0), (1, 0), (0, 0)))
    t = jnp.arange(T)
    means = []
    for g, w in enumerate(POOL_WINDOWS):
        lo = jnp.clip(t - w // 2, 0, T)
        hi = jnp.clip(t - w // 2 + w, 0, T)
        sl = cs[..., g * POOL_GROUP:(g + 1) * POOL_GROUP]
        means.append((sl[:, hi] - sl[:, lo]) / (hi - lo).astype(F32)[None, :, None])
    pooled = (jnp.concatenate(means, -1) - p.astype(F32)).astype(p.dtype)
    pg = pooled.reshape(Bn, T, N_POOL, POOL_GROUP)
    out = jnp.einsum('btgc,gcd->btgd', pg, w_pool) + b_pool
    return out.reshape(Bn, T, C) * pool_scale


def _dwconv(x, w, b):
    K, C = w.shape
    y = lax.conv_general_dilated(x, w[:, None, :].astype(x.dtype), (1,), [(K // 2, K // 2)],
                                 dimension_numbers=('NWC', 'WIO', 'NWC'), feature_group_count=C)
    return y + b


def _ssd(x, dt, A, Bm, Cm, h0):
    Bn, T, H, P = x.shape
    G, N = Bm.shape[2], Bm.shape[3]
    Q = B_CHUNK
    nc = T // Q
    rep = H // G
    xc = x.astype(F32).reshape(Bn, nc, Q, H, P)
    Bc = jnp.repeat(Bm.astype(F32), rep, axis=2).reshape(Bn, nc, Q, H, N)
    Cc = jnp.repeat(Cm.astype(F32), rep, axis=2).reshape(Bn, nc, Q, H, N)
    dtc = dt.reshape(Bn, nc, Q, H)
    Lc = jnp.cumsum(dtc * A, axis=2)
    causal = jnp.tril(jnp.ones((Q, Q), bool))
    seg = Lc[:, :, :, None, :] - Lc[:, :, None, :, :]
    decay = jnp.exp(jnp.where(causal[None, None, :, :, None], seg, -jnp.inf))
    cb = jnp.einsum('bcihn,bcjhn->bcijh', Cc, Bc)
    y_intra = jnp.einsum('bcijh,bcjhp->bcihp', cb * decay * dtc[:, :, None, :, :], xc)
    to_end = jnp.exp(Lc[:, :, -1:, :] - Lc) * dtc
    chunk_states = jnp.einsum('bcjhn,bcjhp->bchpn', Bc * to_end[..., None], xc)
    chunk_decay = jnp.exp(Lc[:, :, -1, :])

    def step(h, inp):
        st, dcy = inp
        return dcy[:, :, None, None] * h + st, h

    h_last, h_start = lax.scan(step, h0.astype(F32),
                               (jnp.moveaxis(chunk_states, 1, 0), jnp.moveaxis(chunk_decay, 1, 0)))
    h_start = jnp.moveaxis(h_start, 0, 1)
    y_inter = jnp.einsum('bcihn,bchpn->bcihp', Cc * jnp.exp(Lc)[..., None], h_start)
    y = (y_intra + y_inter).reshape(Bn, T, H, P)
    return y, h_last


def _ssm_branch(xbc, z, dt_raw, h0_f, h0_b, lp):
    Bn, T, _ = xbc.shape
    xbc = jax.nn.silu(_dwconv(xbc, lp['ssm_conv_w'], lp['ssm_conv_b']))
    nx = B_HEADS * B_HEAD_DIM
    xs, Bm, Cm = jnp.split(xbc, [nx, nx + B_GROUPS * B_STATE], axis=-1)
    x = xs.reshape(Bn, T, B_HEADS, B_HEAD_DIM)
    Bm = Bm.reshape(Bn, T, B_GROUPS, B_STATE)
    Cm = Cm.reshape(Bn, T, B_GROUPS, B_STATE)
    dt = jax.nn.softplus(dt_raw.astype(F32).reshape(Bn, T, 2, B_HEADS) + lp['ssm_dt_bias'].astype(F32))
    A = -jnp.exp(lp['ssm_a_log'].astype(F32))
    y_f, h_f = _ssd(x, dt[:, :, 0], A[0], Bm, Cm, h0_f)
    y_b, h_b = _ssd(jnp.flip(x, 1), jnp.flip(dt[:, :, 1], 1), A[1],
                    jnp.flip(Bm, 1), jnp.flip(Cm, 1), h0_b)
    y = y_f + jnp.flip(y_b, 1) + lp['ssm_d'].astype(F32)[:, None] * x.astype(F32)
    y = y.reshape(Bn, T, D_BRANCH).astype(xbc.dtype)
    return _rms(y * jax.nn.silu(z), lp['ssm_norm_w']), h_f.astype(xbc.dtype), h_b.astype(xbc.dtype)


def _modulate_project(x, cvec, lp):
    mod = jax.nn.silu(cvec) @ lp['w_ada'] + lp['b_ada']
    shift, scale, gate = jnp.split(mod[:, None, :], 3, axis=-1)
    u = _ln(x) * (1.0 + scale) + shift
    return _split_proj(u @ lp['w_in']), gate


def _post_norm_residual(x, mixed, gate, lp, alpha):
    out = (mixed @ lp['w_out']) * gate
    return _ln(alpha * x + out) * lp['ln_g'] + lp['ln_b']


def _context_layer(x, c_ctx, lp, alpha):
    Bn, L, _ = x.shape
    parts, gate = _modulate_project(x, c_ctx[None, :], lp)
    qa, ka, va, ga, xbc, z, dt_raw, pc, gc, qd, kd, vd, gd = parts
    qa = qa.reshape(Bn, L, A_HEADS, A_HEAD_DIM)
    ka = ka.reshape(Bn, L, A_KV_HEADS, A_HEAD_DIM)
    va = va.reshape(Bn, L, A_KV_HEADS, A_HEAD_DIM)
    o_a = _ctx_attention(qa, ka, va, lp['attn_sink']) * jax.nn.silu(ga)
    h0 = jnp.zeros((Bn, B_HEADS, B_HEAD_DIM, B_STATE), x.dtype)
    o_b, h_f, h_b = _ssm_branch(xbc, z, dt_raw, h0, h0, lp)
    o_c = _pool_branch(pc, lp['pool_w'], lp['pool_b'], lp['pool_scale']) * jax.nn.silu(gc)
    qd = qd.reshape(Bn, L, D_HEADS, D_HEAD_DIM)
    kd = kd.reshape(Bn, L, D_HEADS, D_HEAD_DIM)
    vd = vd.reshape(Bn, L, D_HEADS, D_HEAD_DIM)
    o_d = _ctx_attention(qd, kd, vd, None) * jax.nn.silu(gd)
    mixed = jnp.concatenate([o_a, o_b, o_c, o_d], -1)
    return _post_norm_residual(x, mixed, gate, lp, alpha), (ka, va, kd, vd, h_f, h_b)


def _latent_layer(x, c, lp, ka_c, va_c, kd_c, vd_c, hf0, hb0, alpha):
    Bn, T, _ = x.shape
    parts, gate = _modulate_project(x, c, lp)
    qa, ka, va, ga, xbc, z, dt_raw, pc, gc, qd, kd, vd, gd = parts
    rows, cols = _grid_pos(T)
    qa = _rope_axial(qa.reshape(Bn, T, A_HEADS, A_HEAD_DIM), rows, cols)
    ka = _rope_axial(ka.reshape(Bn, T, A_KV_HEADS, A_HEAD_DIM), rows, cols)
    va = va.reshape(Bn, T, A_KV_HEADS, A_HEAD_DIM)
    o_a = _window_attention(qa, ka, va, ka_c, va_c, lp['attn_sink']) * jax.nn.silu(ga)
    o_b, _, _ = _ssm_branch(xbc, z, dt_raw, hf0, hb0, lp)
    o_c = _pool_branch(pc, lp['pool_w'], lp['pool_b'], lp['pool_scale']) * jax.nn.silu(gc)
    qd = qd.reshape(Bn, T, D_HEADS, D_HEAD_DIM)
    kd = kd.reshape(Bn, T, D_HEADS, D_HEAD_DIM)
    vd = vd.reshape(Bn, T, D_HEADS, D_HEAD_DIM)
    o_d = _neighbourhood_attention(qd, kd, vd, kd_c, vd_c, lp['na_rpb']) * jax.nn.silu(gd)
    mixed = jnp.concatenate([o_a, o_b, o_c, o_d], -1)
    return _post_norm_residual(x, mixed, gate, lp, alpha)


def setup_inputs(seed: int = 0) -> dict:
    key = jax.random.key(seed)
    ks = jax.random.split(key, 27)

    def nrm(k, shape, s=1.0):
        return jax.random.normal(k, shape, jnp.float32) * s

    out_scale = (8.0 * DEPTH) ** -0.25
    dt_init = jnp.exp(jax.random.uniform(ks[20], (DEPTH, 2, B_HEADS), jnp.float32,
                                         math.log(1e-3), math.log(1e-1)))
    return {
        'x_prompt': nrm(ks[0], (BATCH, SEQ, D_MODEL)),
        'x_sample': nrm(ks[1], (DEC_BATCH, DEC_SEQ, D_MODEL)),
        'cache_attn_k': nrm(ks[2], (DEC_BATCH, DEPTH, PAST_LEN, A_KV_HEADS, A_HEAD_DIM)),
        'cache_attn_v': nrm(ks[3], (DEC_BATCH, DEPTH, PAST_LEN, A_KV_HEADS, A_HEAD_DIM)),
        'cache_na_k': nrm(ks[4], (DEC_BATCH, DEPTH, PAST_LEN, D_HEADS, D_HEAD_DIM)),
        'cache_na_v': nrm(ks[5], (DEC_BATCH, DEPTH, PAST_LEN, D_HEADS, D_HEAD_DIM)),
        'state_ssm_fwd': nrm(ks[6], (DEC_BATCH, DEPTH, B_HEADS, B_HEAD_DIM, B_STATE), 0.5),
        'state_ssm_bwd': nrm(ks[7], (DEC_BATCH, DEPTH, B_HEADS, B_HEAD_DIM, B_STATE), 0.5),
        'c': nrm(ks[8], (DEC_BATCH, D_MODEL)),
        'c_ctx': nrm(ks[9], (D_MODEL,)),
        'w_ada': nrm(ks[10], (DEPTH, D_MODEL, 3 * D_MODEL), 0.5 * D_MODEL ** -0.5),
        'b_ada': nrm(ks[11], (DEPTH, 3 * D_MODEL), 0.02),
        'w_in': nrm(ks[12], (DEPTH, D_MODEL, D_PROJ), D_MODEL ** -0.5),
        'w_out': nrm(ks[13], (DEPTH, D_MIX, D_MODEL), out_scale * D_MIX ** -0.5),
        'ln_g': 1.0 + nrm(ks[14], (DEPTH, D_MODEL), 0.02),
        'ln_b': nrm(ks[15], (DEPTH, D_MODEL), 0.02),
        'attn_sink': nrm(ks[16], (DEPTH, A_HEADS), 0.5),
        'ssm_conv_w': nrm(ks[17], (DEPTH, B_CONV, B_XBC), B_CONV ** -0.5),
        'ssm_conv_b': nrm(ks[18], (DEPTH, B_XBC), 0.02),
        'ssm_a_log': jnp.log(jax.random.uniform(ks[19], (DEPTH, 2, B_HEADS), jnp.float32, 1.0, 16.0)),
        'ssm_dt_bias': dt_init + jnp.log(-jnp.expm1(-dt_init)),
        'ssm_d': 1.0 + nrm(ks[21], (DEPTH, B_HEADS), 0.1),
        'ssm_norm_w': 1.0 + nrm(ks[22], (DEPTH, D_BRANCH), 0.02),
        'pool_w': nrm(ks[23], (DEPTH, N_POOL, POOL_GROUP, POOL_GROUP), POOL_GROUP ** -0.5),
        'pool_b': nrm(ks[24], (DEPTH, N_POOL, POOL_GROUP), 0.02),
        'pool_scale': 1.0 + nrm(ks[25], (DEPTH, D_BRANCH), 0.05),
        'na_rpb': nrm(ks[26], (DEPTH, D_HEADS, 2 * NA_KH - 1, 2 * NA_KW - 1), 0.1),
    }


def reference(x_prompt, x_sample, cache_attn_k, cache_attn_v, cache_na_k, cache_na_v,
              state_ssm_fwd, state_ssm_bwd, c, c_ctx, w_ada, b_ada, w_in, w_out, ln_g, ln_b,
              attn_sink, ssm_conv_w, ssm_conv_b, ssm_a_log, ssm_dt_bias, ssm_d, ssm_norm_w,
              pool_w, pool_b, pool_scale, na_rpb):
    alpha = (2.0 * DEPTH) ** 0.25
    y_prompt = x_prompt
    y_sample = x_sample
    ctx_states = []
    for l in range(DEPTH):
        lp = {'w_ada': w_ada[l], 'b_ada': b_ada[l], 'w_in': w_in[l], 'w_out': w_out[l],
              'ln_g': ln_g[l], 'ln_b': ln_b[l], 'attn_sink': attn_sink[l],
              'ssm_conv_w': ssm_conv_w[l], 'ssm_conv_b': ssm_conv_b[l], 'ssm_a_log': ssm_a_log[l],
              'ssm_dt_bias': ssm_dt_bias[l], 'ssm_d': ssm_d[l], 'ssm_norm_w': ssm_norm_w[l],
              'pool_w': pool_w[l], 'pool_b': pool_b[l], 'pool_scale': pool_scale[l],
              'na_rpb': na_rpb[l]}
        y_prompt, st = _context_layer(y_prompt, c_ctx, lp, alpha)
        ctx_states.append(st)
        y_sample = _latent_layer(y_sample, c, lp, cache_attn_k[:, l], cache_attn_v[:, l],
                                 cache_na_k[:, l], cache_na_v[:, l],
                                 state_ssm_fwd[:, l], state_ssm_bwd[:, l], alpha)
    new_attn_k = jnp.stack([s[0] for s in ctx_states], axis=1)
    new_attn_v = jnp.stack([s[1] for s in ctx_states], axis=1)
    new_na_k = jnp.stack([s[2] for s in ctx_states], axis=1)
    new_na_v = jnp.stack([s[3] for s in ctx_states], axis=1)
    new_ssm_fwd = jnp.stack([s[4] for s in ctx_states], axis=1)
    new_ssm_bwd = jnp.stack([s[5] for s in ctx_states], axis=1)
    return (y_prompt, y_sample, new_attn_k, new_attn_v, new_na_k, new_na_v, new_ssm_fwd, new_ssm_bwd)
```

```python
import functools

import jax
import jax.numpy as jnp
from jax import lax
from jax.experimental import pallas as pl
from jax.experimental.pallas import tpu as pltpu

F32 = jnp.float32
BF16 = jnp.bfloat16

D_MODEL = 2048
BATCH = 32
SEQ = 256
DEPTH = 2
DEC_BATCH = 4
DEC_SEQ = 1024
PAST_LEN = 256
GRID_W = 64
D_BRANCH = 512
HEAD_DIM = 128
A_HEADS = 4
A_KV_HEADS = 2
A_WINDOW = 128
ROPE_THETA = 10000.0
B_HEADS = 8
B_HEAD_DIM = 64
B_STATE = 128
B_CONV = 5
CHUNK = 128
POOL_WINDOWS = (2, 4, 8, 16)
D_HEADS = 4
NA_KH = 8
NA_KW = 16
LN_EPS = 1e-6
NEG_INF = -1e30
ATT_SCALE = HEAD_DIM ** -0.5

D_MAIN = 6144
DT_OFF = 3072
DT_COLS = 16
LANES = 128
ROW_PAD = 8
VMEM_LIMIT = 48 * 1024 * 1024

TRANS_B = (((1,), (1,)), ((), ()))


def _silu(x):
    return x / (1.0 + jnp.exp(-x))


def _split_bf16(a, parts):
    out = []
    rem = a
    for _ in range(parts):
        hi = rem.astype(BF16)
        out.append(hi)
        rem = rem - hi.astype(F32)
    return out


def _dot01_lhs(m01, a, parts=3):
    acc = None
    for p in _split_bf16(a, parts):
        t = jnp.dot(m01, p, preferred_element_type=F32)
        acc = t if acc is None else acc + t
    return acc


def _dot01_rhs(a, m01, parts=2):
    acc = None
    for p in _split_bf16(a, parts):
        t = jnp.dot(p, m01, preferred_element_type=F32)
        acc = t if acc is None else acc + t
    return acc


def _softmax_av(scores, values, sink=None):
    m = None
    for s in scores:
        bm = jnp.max(s, axis=-1, keepdims=True)
        m = bm if m is None else jnp.maximum(m, bm)
    if sink is not None:
        m = jnp.maximum(m, sink)
    den = None
    acc = None
    for s, v in zip(scores, values):
        p = jnp.exp(s - m)
        d = jnp.sum(p, axis=-1, keepdims=True)
        o = jnp.dot(p.astype(BF16), v, preferred_element_type=F32)
        den = d if den is None else den + d
        acc = o if acc is None else acc + o
    if sink is not None:
        den = den + jnp.exp(sink - m)
    return acc / den


def _ada_kernel(cv_ref, w_ref, b_ref, o_ref):
    a = _silu(cv_ref[...]).astype(BF16)
    w = w_ref[0].astype(BF16)
    o_ref[0] = jnp.dot(a, w, preferred_element_type=F32) + b_ref[0]


def _ada(cv, w_ada, b_ada):
    tn = 1024
    n = 3 * D_MODEL
    return pl.pallas_call(
        _ada_kernel,
        grid=(DEPTH, n // tn),
        in_specs=[
            pl.BlockSpec((16, D_MODEL), lambda l, j: (0, 0)),
            pl.BlockSpec((1, D_MODEL, tn), lambda l, j: (l, 0, j)),
            pl.BlockSpec((1, 1, tn), lambda l, j: (l, 0, j)),
        ],
        out_specs=pl.BlockSpec((1, 16, tn), lambda l, j: (l, 0, j)),
        out_shape=jax.ShapeDtypeStruct((DEPTH, 16, n), F32),
        compiler_params=pltpu.CompilerParams(vmem_limit_bytes=VMEM_LIMIT),
        name="ada_mod",
    )(cv, w_ada, b_ada.reshape(DEPTH, 1, n))


IN_TM = 1024
IN_TN = 1024
IN_RC = 128


def _inproj_kernel(x_ref, mod_ref, w_ref, wdt_ref, o_ref, dt_ref, u_ref):
    @pl.when(pl.program_id(1) == 0)
    def _():
        shift = mod_ref[0, :, 0:D_MODEL]
        scale1 = 1.0 + mod_ref[0, :, D_MODEL:2 * D_MODEL]

        def body(r, carry):
            rows = pl.ds(pl.multiple_of(r * IN_RC, IN_RC), IN_RC)
            xf = x_ref[rows, :]
            mu = jnp.mean(xf, axis=-1, keepdims=True)
            xc = xf - mu
            var = jnp.mean(xc * xc, axis=-1, keepdims=True)
            u = xc * lax.rsqrt(var + LN_EPS) * scale1 + shift
            u_ref[rows, :] = u.astype(BF16)
            return carry

        lax.fori_loop(0, IN_TM // IN_RC, body, 0)
        dt_ref[...] = jnp.dot(u_ref[...], wdt_ref[...], preferred_element_type=F32)

    o_ref[...] = jnp.dot(u_ref[...], w_ref[...], preferred_element_type=F32)


def _inproj(x2d, mod3, w_main, w_dt, mod_row):
    m = x2d.shape[0]
    return pl.pallas_call(
        _inproj_kernel,
        grid=(m // IN_TM, D_MAIN // IN_TN),
        in_specs=[
            pl.BlockSpec((IN_TM, D_MODEL), lambda i, j: (i, 0)),
            pl.BlockSpec((1, 1, 3 * D_MODEL), lambda i, j: (mod_row(i), 0, 0)),
            pl.BlockSpec((D_MODEL, IN_TN), lambda i, j: (0, j)),
            pl.BlockSpec((D_MODEL, LANES), lambda i, j: (0, 0)),
        ],
        out_specs=[
            pl.BlockSpec((IN_TM, IN_TN), lambda i, j: (i, j)),
            pl.BlockSpec((IN_TM, LANES), lambda i, j: (i, 0)),
        ],
        out_shape=[
            jax.ShapeDtypeStruct((m, D_MAIN), F32),
            jax.ShapeDtypeStruct((m, LANES), F32),
        ],
        scratch_shapes=[pltpu.VMEM((IN_TM, D_MODEL), BF16)],
        compiler_params=pltpu.CompilerParams(
            dimension_semantics=("arbitrary", "arbitrary"), vmem_limit_bytes=VMEM_LIMIT),
        name="in_proj",
    )(x2d, mod3, w_main, w_dt)


OUT_TM = 512
OUT_RC = 128


def _outproj_kernel(ma_ref, mb_ref, mc_ref, md_ref, x_ref, mod_ref, w_ref, g_ref, b_ref, o_ref, *, alpha):
    gate = mod_ref[0, :, 2 * D_MODEL:3 * D_MODEL]
    parts = (ma_ref, mb_ref, mc_ref, md_ref)

    def body(r, carry):
        rows = pl.ds(pl.multiple_of(r * OUT_RC, OUT_RC), OUT_RC)
        acc = None
        for g, p_ref in enumerate(parts):
            t = jnp.dot(p_ref[rows, :], w_ref[g * D_BRANCH:(g + 1) * D_BRANCH, :], preferred_element_type=F32)
            acc = t if acc is None else acc + t
        z = alpha * x_ref[rows, :] + acc * gate
        mu = jnp.mean(z, axis=-1, keepdims=True)
        zc = z - mu
        var = jnp.mean(zc * zc, axis=-1, keepdims=True)
        o_ref[rows, :] = zc * lax.rsqrt(var + LN_EPS) * g_ref[...] + b_ref[...]
        return carry

    lax.fori_loop(0, OUT_TM // OUT_RC, body, 0)


def _outproj(mixed, x2d, mod3, w_o, ln_g, ln_b, mod_row, alpha):
    m = x2d.shape[0]
    mspec = pl.BlockSpec((OUT_TM, D_BRANCH), lambda i: (i, 0))
    return pl.pallas_call(
        functools.partial(_outproj_kernel, alpha=alpha),
        grid=(m // OUT_TM,),
        in_specs=[
            mspec, mspec, mspec, mspec,
            pl.BlockSpec((OUT_TM, D_MODEL), lambda i: (i, 0)),
            pl.BlockSpec((1, 1, 3 * D_MODEL), lambda i: (mod_row(i), 0, 0)),
            pl.BlockSpec((D_MODEL, D_MODEL), lambda i: (0, 0)),
            pl.BlockSpec((1, D_MODEL), lambda i: (0, 0)),
            pl.BlockSpec((1, D_MODEL), lambda i: (0, 0)),
        ],
        out_specs=pl.BlockSpec((OUT_TM, D_MODEL), lambda i: (i, 0)),
        out_shape=jax.ShapeDtypeStruct((m, D_MODEL), F32),
        compiler_params=pltpu.CompilerParams(vmem_limit_bytes=VMEM_LIMIT),
        name="out_proj",
    )(*mixed, x2d, mod3, w_o, ln_g.reshape(1, D_MODEL), ln_b.reshape(1, D_MODEL))


def _ctx_attn_kernel(sink_ref, p_ref, o_ref, *, n_q, n_kv, use_sink):
    grp = n_q // n_kv
    k_off = n_q * HEAD_DIM
    v_off = k_off + n_kv * HEAD_DIM
    g_off = v_off + n_kv * HEAD_DIM
    for kk in range(n_kv):
        k = p_ref[:, k_off + kk * HEAD_DIM:k_off + (kk + 1) * HEAD_DIM].astype(BF16)
        v = p_ref[:, v_off + kk * HEAD_DIM:v_off + (kk + 1) * HEAD_DIM].astype(BF16)
        for g in range(grp):
            h = kk * grp + g
            cols = slice(h * HEAD_DIM, (h + 1) * HEAD_DIM)
            q = p_ref[:, cols].astype(BF16)
            s = lax.dot_general(q, k, TRANS_B, preferred_element_type=F32) * ATT_SCALE
            o = _softmax_av([s], [v], sink_ref[h] if use_sink else None)
            gate = p_ref[:, g_off + h * HEAD_DIM:g_off + (h + 1) * HEAD_DIM]
            o_ref[:, cols] = (o * _silu(gate)).astype(BF16)


def _ctx_attn(proj, sink, *, n_q, n_kv, col_block, width, use_sink):
    return pl.pallas_call(
        functools.partial(_ctx_attn_kernel, n_q=n_q, n_kv=n_kv, use_sink=use_sink),
        grid=(BATCH,),
        in_specs=[
            pl.BlockSpec(memory_space=pltpu.SMEM),
            pl.BlockSpec((SEQ, width), lambda b: (b, col_block)),
        ],
        out_specs=pl.BlockSpec((SEQ, n_q * HEAD_DIM), lambda b: (b, 0)),
        out_shape=jax.ShapeDtypeStruct((BATCH * SEQ, n_q * HEAD_DIM), BF16),
        compiler_params=pltpu.CompilerParams(vmem_limit_bytes=VMEM_LIMIT),
        name="ctx_attn",
    )(sink, proj)


def _ssm_kernel(*refs, seq, has_h0, emit_state):
    nc = seq // CHUNK
    it = iter(refs)
    p_ref, dt_ref, cw_ref, cb_ref, dtb_ref, alog_ref, dskip_ref, nw_ref = (next(it) for _ in range(8))
    h0f_ref = next(it) if has_h0 else None
    h0b_ref = next(it) if has_h0 else None
    o_ref = next(it)
    hf_ref = next(it) if emit_state else None
    hb_ref = next(it) if emit_state else None
    xpad, xbc_s, y_s, eb_s, sb_s, db_s, ht_s, ex_s = (next(it) for _ in range(8))

    n_x = B_HEADS * B_HEAD_DIM
    n_xbc = 2 * n_x

    r_i = lax.broadcasted_iota(jnp.int32, (LANES, 2 * n_x), 0)
    c_i = lax.broadcasted_iota(jnp.int32, (LANES, 2 * n_x), 1)
    ex_s[...] = jnp.where(r_i == lax.shift_right_logical(c_i, 6), 1.0, 0.0).astype(BF16)

    zero_rows = jnp.zeros((ROW_PAD, n_xbc), F32)
    xpad[0:ROW_PAD, :] = zero_rows
    xpad[seq + ROW_PAD:seq + 2 * ROW_PAD, :] = zero_rows

    def copy_body(c, carry):
        r0 = pl.multiple_of(c * CHUNK, CHUNK)
        xpad[pl.ds(pl.multiple_of(r0 + ROW_PAD, ROW_PAD), CHUNK), :] = p_ref[pl.ds(r0, CHUNK), 0:n_xbc]
        return carry

    lax.fori_loop(0, nc, copy_body, 0)

    def load_state(h_ref):
        for blk in range(4):
            cols = slice(blk * LANES, (blk + 1) * LANES)
            if h_ref is None:
                ht_s[:, cols] = jnp.zeros((B_STATE, LANES), F32)
            else:
                ht_s[:, cols] = h_ref[0, cols, :].T

    def store_state(h_ref):
        for blk in range(4):
            cols = slice(blk * LANES, (blk + 1) * LANES)
            h_ref[0, cols, :] = ht_s[:, cols].T

    load_state(h0f_ref)

    def chunk_body(c, carry):
        r0 = pl.multiple_of(c * CHUNK, CHUNK)
        rows = pl.ds(r0, CHUNK)
        win = xpad[pl.ds(r0, CHUNK + 2 * ROW_PAD), :]
        acc = jnp.broadcast_to(cb_ref[...], (CHUNK, n_xbc))
        for k in range(B_CONV):
            sh = (B_CONV // 2 - k) % (CHUNK + 2 * ROW_PAD)
            rolled = win if sh == 0 else pltpu.roll(win, sh, axis=0)
            acc = acc + cw_ref[k:k + 1, :] * rolled[ROW_PAD:ROW_PAD + CHUNK, :]
        xbc = _silu(acc)
        xbc_s[rows, :] = xbc
        xs = xbc[:, 0:n_x]
        bm = xbc[:, n_x:n_x + 2 * B_STATE]
        cm = xbc[:, n_x + 2 * B_STATE:n_xbc]

        dtr = dt_ref[rows, :] + dtb_ref[...]
        dt = jnp.maximum(dtr, 0.0) + jnp.log1p(jnp.exp(-jnp.abs(dtr)))
        a = dt * (-jnp.exp(alog_ref[...]))
        ii = lax.broadcasted_iota(jnp.int32, (CHUNK, CHUNK), 0)
        jj = lax.broadcasted_iota(jnp.int32, (CHUNK, CHUNK), 1)
        tril = jj <= ii
        triu = jj >= ii
        lc_f = _dot01_lhs(jnp.where(tril, 1.0, 0.0).astype(BF16), a)
        lc_b = _dot01_lhs(jnp.where(triu, 1.0, 0.0).astype(BF16), a)
        fwd_lane = jj < B_HEADS
        lc = jnp.where(fwd_lane, lc_f, lc_b)
        lend = jnp.where(fwd_lane[0:1, :], lc[CHUNK - 1:CHUNK, :], lc[0:1, :])
        toend = jnp.exp(lend - lc) * dt
        e_exp = _dot01_rhs(jnp.exp(lc), ex_s[...])
        w_exp = _dot01_rhs(toend, ex_s[...])
        d_exp = _dot01_rhs(jnp.broadcast_to(jnp.exp(lend), (ROW_PAD, LANES)), ex_s[...])
        lc_t = lc.T
        dt_t = dt.T
        eb_s[rows, :] = e_exp[:, n_x:2 * n_x]
        db_s[c] = d_exp[:, n_x:2 * n_x]

        lane_g = lax.broadcasted_iota(jnp.int32, (CHUNK, 2 * LANES), 1)
        for g in range(2):
            gcols = slice(g * 2 * LANES, (g + 1) * 2 * LANES)
            cg = cm[:, g * B_STATE:(g + 1) * B_STATE].astype(BF16)
            bg = bm[:, g * B_STATE:(g + 1) * B_STATE]
            cb = lax.dot_general(cg, bg.astype(BF16), TRANS_B, preferred_element_type=F32)
            bg_t = bg.T.astype(BF16)
            xg = xs[:, gcols]
            xblk = jnp.concatenate(
                [jnp.where((lane_g >= hh * B_HEAD_DIM) & (lane_g < (hh + 1) * B_HEAD_DIM), xg, 0.0).astype(BF16)
                 for hh in range(4)], axis=0)
            for d in range(2):
                mask = tril if d == 0 else triu
                ms = []
                for hh in range(4):
                    col = d * B_HEADS + g * 4 + hh
                    seg = lc[:, col:col + 1] - lc_t[col:col + 1, :]
                    dec = jnp.exp(jnp.where(mask, seg, -jnp.inf))
                    ms.append((cb * dec * dt_t[col:col + 1, :]).astype(BF16))
                y_in = jnp.dot(jnp.concatenate(ms, axis=1), xblk, preferred_element_type=F32)
                wg = w_exp[:, d * n_x + g * 2 * LANES:d * n_x + (g + 1) * 2 * LANES]
                st = jnp.dot(bg_t, (xg * wg).astype(BF16), preferred_element_type=F32)
                if d == 0:
                    h_prev = ht_s[:, gcols]
                    y_x = e_exp[:, gcols] * jnp.dot(cg, h_prev.astype(BF16), preferred_element_type=F32)
                    y_s[rows, gcols] = y_in + y_x
                    ht_s[:, gcols] = d_exp[0:1, gcols] * h_prev + st
                else:
                    y_s[rows, gcols] = y_s[rows, gcols] + y_in
                    sb_s[c, :, gcols] = st
        return carry

    lax.fori_loop(0, nc, chunk_body, 0)
    if emit_state:
        store_state(hf_ref)
    load_state(h0b_ref)

    def bwd_body(i, carry):
        c = nc - 1 - i
        r0 = pl.multiple_of(c * CHUNK, CHUNK)
        rows = pl.ds(r0, CHUNK)
        for g in range(2):
            gcols = slice(g * 2 * LANES, (g + 1) * 2 * LANES)
            cg = xbc_s[rows, n_x + 2 * B_STATE + g * B_STATE:n_x + 2 * B_STATE + (g + 1) * B_STATE].astype(BF16)
            h_prev = ht_s[:, gcols]
            y_x = eb_s[rows, gcols] * jnp.dot(cg, h_prev.astype(BF16), preferred_element_type=F32)
            y_s[rows, gcols] = y_s[rows, gcols] + y_x
            ht_s[:, gcols] = db_s[c, 0:1, gcols] * h_prev + sb_s[c, :, gcols]
        return carry

    lax.fori_loop(0, nc, bwd_body, 0)
    if emit_state:
        store_state(hb_ref)

    def out_body(c, carry):
        rows = pl.ds(pl.multiple_of(c * CHUNK, CHUNK), CHUNK)
        y = y_s[rows, :] + dskip_ref[...] * xbc_s[rows, 0:n_x]
        yz = y * _silu(p_ref[rows, n_xbc:n_xbc + n_x])
        ms = jnp.mean(yz * yz, axis=-1, keepdims=True)
        o_ref[rows, :] = (yz * lax.rsqrt(ms + LN_EPS) * nw_ref[...]).astype(BF16)
        return carry

    lax.fori_loop(0, nc, out_body, 0)


def _ssm(proj, dt, lp, *, n_seq, seq, h0f=None, h0b=None):
    has_h0 = h0f is not None
    emit_state = not has_h0
    nc = seq // CHUNK
    n_x = B_HEADS * B_HEAD_DIM
    const2 = lambda b: (0, 0)
    in_specs = [
        pl.BlockSpec((seq, 3 * n_x), lambda b: (b, 1)),
        pl.BlockSpec((seq, LANES), lambda b: (b, 0)),
        pl.BlockSpec((ROW_PAD, 2 * n_x), const2),
        pl.BlockSpec((1, 2 * n_x), const2),
        pl.BlockSpec((1, LANES), const2),
        pl.BlockSpec((1, LANES), const2),
        pl.BlockSpec((1, n_x), const2),
        pl.BlockSpec((1, n_x), const2),
    ]
    args = [proj, dt, lp["conv_w"], lp["conv_b"], lp["dt_bias"], lp["a_log"], lp["d_skip"], lp["norm_w"]]
    state_spec = pl.BlockSpec((1, n_x, B_STATE), lambda b: (b, 0, 0))
    if has_h0:
        in_specs += [state_spec, state_spec]
        args += [h0f, h0b]
    out_specs = [pl.BlockSpec((seq, n_x), lambda b: (b, 0))]
    out_shape = [jax.ShapeDtypeStruct((n_seq * seq, n_x), BF16)]
    if emit_state:
        out_specs += [state_spec, state_spec]
        out_shape += [jax.ShapeDtypeStruct((n_seq, n_x, B_STATE), F32)] * 2
    return pl.pallas_call(
        functools.partial(_ssm_kernel, seq=seq, has_h0=has_h0, emit_state=emit_state),
        grid=(n_seq,),
        in_specs=in_specs,
        out_specs=out_specs,
        out_shape=out_shape,
        scratch_shapes=[
            pltpu.VMEM((seq + 2 * ROW_PAD, 2 * n_x), F32),
            pltpu.VMEM((seq, 2 * n_x), F32),
            pltpu.VMEM((seq, n_x), F32),
            pltpu.VMEM((seq, n_x), F32),
            pltpu.VMEM((nc, B_STATE, n_x), F32),
            pltpu.VMEM((nc, ROW_PAD, n_x), F32),
            pltpu.VMEM((B_STATE, n_x), F32),
            pltpu.VMEM((LANES, 2 * n_x), BF16),
        ],
        compiler_params=pltpu.CompilerParams(vmem_limit_bytes=VMEM_LIMIT),
        name="ssm",
    )(*args)


def _pool_kernel(p_ref, w_ref, b_ref, sc_ref, o_ref, ppad, *, seq):
    nb = seq // CHUNK
    zero_blk = jnp.zeros((CHUNK, D_BRANCH), F32)
    ppad[0:CHUNK, :] = zero_blk
    ppad[seq + CHUNK:seq + 2 * CHUNK, :] = zero_blk

    def copy_body(c, carry):
        r0 = pl.multiple_of(c * CHUNK, CHUNK)
        ppad[pl.ds(pl.multiple_of(r0 + CHUNK, CHUNK), CHUNK), :] = p_ref[pl.ds(r0, CHUNK), 0:D_BRANCH]
        return carry

    lax.fori_loop(0, nb, copy_body, 0)

    def blk_body(rb, carry):
        r0 = pl.multiple_of(rb * CHUNK, CHUNK)
        rows = pl.ds(r0, CHUNK)
        ii = lax.broadcasted_iota(jnp.int32, (CHUNK, 3 * CHUNK), 0)
        jj = lax.broadcasted_iota(jnp.int32, (CHUNK, 3 * CHUNK), 1)
        rel = jj - CHUNK - ii
        t = r0 + lax.broadcasted_iota(jnp.int32, (CHUNK, 1), 0)
        for g, w in enumerate(POOL_WINDOWS):
            cols = slice(g * LANES, (g + 1) * LANES)
            band = jnp.where((rel >= -(w // 2)) & (rel < w - w // 2), 1.0, 0.0).astype(BF16)
            win = ppad[pl.ds(r0, 3 * CHUNK), cols]
            tot = _dot01_lhs(band, win)
            lo = jnp.clip(t - w // 2, 0, seq)
            hi = jnp.clip(t - w // 2 + w, 0, seq)
            pooled = tot / (hi - lo).astype(F32) - p_ref[rows, cols]
            out = jnp.dot(pooled.astype(BF16), w_ref[g].astype(BF16), preferred_element_type=F32) + b_ref[g]
            gate = p_ref[rows, D_BRANCH + g * LANES:D_BRANCH + (g + 1) * LANES]
            o_ref[rows, cols] = (out * sc_ref[:, cols] * _silu(gate)).astype(BF16)
        return carry

    lax.fori_loop(0, nb, blk_body, 0)


def _pool(proj, lp, *, n_seq, seq):
    return pl.pallas_call(
        functools.partial(_pool_kernel, seq=seq),
        grid=(n_seq,),
        in_specs=[
            pl.BlockSpec((seq, 2 * D_BRANCH), lambda b: (b, 3)),
            pl.BlockSpec((4, LANES, LANES), lambda b: (0, 0, 0)),
            pl.BlockSpec((4, 1, LANES), lambda b: (0, 0, 0)),
            pl.BlockSpec((1, D_BRANCH), lambda b: (0, 0)),
        ],
        out_specs=pl.BlockSpec((seq, D_BRANCH), lambda b: (b, 0)),
        out_shape=jax.ShapeDtypeStruct((n_seq * seq, D_BRANCH), BF16),
        scratch_shapes=[pltpu.VMEM((seq + 2 * CHUNK, D_BRANCH), F32)],
        compiler_params=pltpu.CompilerParams(vmem_limit_bytes=VMEM_LIMIT),
        name="pool",
    )(proj, lp["pool_w"], lp["pool_b"], lp["pool_scale"])


def _rope(x, cos, sin):
    lane = lax.broadcasted_iota(jnp.int32, x.shape, 1)
    first = (lane & 63) < 32
    swapped = jnp.where(first, pltpu.roll(x, 96, axis=1), pltpu.roll(x, 32, axis=1))
    return x * cos + swapped * sin


def _win_attn_kernel(sink_ref, p_ref, kc_ref, vc_ref, cos_ref, sin_ref, o_ref, q_s, k_s, v_s):
    seq = DEC_SEQ
    nb = seq // CHUNK
    kv_w = A_KV_HEADS * HEAD_DIM
    k_off = A_HEADS * HEAD_DIM
    v_off = k_off + kv_w
    g_off = v_off + kv_w
    zero_blk = jnp.zeros((CHUNK, kv_w), BF16)
    for s in (k_s, v_s):
        s[0:CHUNK, :] = zero_blk
        s[seq + CHUNK:seq + 2 * CHUNK, :] = zero_blk

    def prep_body(c, carry):
        r0 = pl.multiple_of(c * CHUNK, CHUNK)
        rows = pl.ds(r0, CHUNK)
        prow = pl.ds(pl.multiple_of(r0 + CHUNK, CHUNK), CHUNK)
        cos = cos_ref[rows, :]
        sin = sin_ref[rows, :]
        for h in range(A_HEADS):
            cols = slice(h * HEAD_DIM, (h + 1) * HEAD_DIM)
            q_s[rows, cols] = _rope(p_ref[rows, cols], cos, sin).astype(BF16)
        for kk in range(A_KV_HEADS):
            cols = slice(kk * HEAD_DIM, (kk + 1) * HEAD_DIM)
            k_s[prow, cols] = _rope(p_ref[rows, k_off + kk * HEAD_DIM:k_off + (kk + 1) * HEAD_DIM], cos, sin).astype(BF16)
        v_s[prow, :] = p_ref[rows, v_off:v_off + kv_w].astype(BF16)
        return carry

    lax.fori_loop(0, nb, prep_body, 0)

    grp = A_HEADS // A_KV_HEADS

    def blk_body(n, carry):
        r0 = pl.multiple_of(n * CHUNK, CHUNK)
        rows = pl.ds(r0, CHUNK)
        win = pl.ds(r0, 3 * CHUNK)
        ii = lax.broadcasted_iota(jnp.int32, (grp * CHUNK, 3 * CHUNK), 0)
        jj = lax.broadcasted_iota(jnp.int32, (grp * CHUNK, 3 * CHUNK), 1)
        rel = jj - CHUNK - (ii & (CHUNK - 1))
        kpos = r0 - CHUNK + jj
        valid = (rel >= -A_WINDOW) & (rel <= A_WINDOW) & (kpos >= 0) & (kpos < seq)
        first_head = lax.broadcasted_iota(jnp.int32, (grp * CHUNK, 1), 0) < CHUNK
        for kk in range(A_KV_HEADS):
            kcols = slice(kk * HEAD_DIM, (kk + 1) * HEAD_DIM)
            h0 = kk * grp
            q = jnp.concatenate([q_s[rows, (h0 + g) * HEAD_DIM:(h0 + g + 1) * HEAD_DIM] for g in range(grp)], axis=0)
            s_loc = lax.dot_general(q, k_s[win, kcols], TRANS_B, preferred_element_type=F32) * ATT_SCALE
            s_loc = jnp.where(valid, s_loc, NEG_INF)
            s_ctx = lax.dot_general(q, kc_ref[0, 0, :, kcols].astype(BF16), TRANS_B,
                                    preferred_element_type=F32) * ATT_SCALE
            sink = jnp.where(first_head, sink_ref[h0], sink_ref[h0 + 1])
            o = _softmax_av([s_loc, s_ctx], [v_s[win, kcols], vc_ref[0, 0, :, kcols].astype(BF16)], sink)
            for g in range(grp):
                cols = slice((h0 + g) * HEAD_DIM, (h0 + g + 1) * HEAD_DIM)
                gate = p_ref[rows, g_off + (h0 + g) * HEAD_DIM:g_off + (h0 + g + 1) * HEAD_DIM]
                o_ref[rows, cols] = (o[g * CHUNK:(g + 1) * CHUNK, :] * _silu(gate)).astype(BF16)
        return carry

    lax.fori_loop(0, nb, blk_body, 0)


def _win_attn(proj, sink, kc, vc, cos_t, sin_t, layer):
    seq = DEC_SEQ
    kv_w = A_KV_HEADS * HEAD_DIM
    cache_spec = pl.BlockSpec((1, 1, PAST_LEN, kv_w), lambda b: (b, layer, 0, 0))
    tab_spec = pl.BlockSpec((seq, HEAD_DIM), lambda b: (0, 0))
    return pl.pallas_call(
        _win_attn_kernel,
        grid=(DEC_BATCH,),
        in_specs=[
            pl.BlockSpec(memory_space=pltpu.SMEM),
            pl.BlockSpec((seq, 3 * D_BRANCH), lambda b: (b, 0)),
            cache_spec, cache_spec, tab_spec, tab_spec,
        ],
        out_specs=pl.BlockSpec((seq, D_BRANCH), lambda b: (b, 0)),
        out_shape=jax.ShapeDtypeStruct((DEC_BATCH * seq, D_BRANCH), BF16),
        scratch_shapes=[
            pltpu.VMEM((seq, A_HEADS * HEAD_DIM), BF16),
            pltpu.VMEM((seq + 2 * CHUNK, kv_w), BF16),
            pltpu.VMEM((seq + 2 * CHUNK, kv_w), BF16),
        ],
        compiler_params=pltpu.CompilerParams(vmem_limit_bytes=VMEM_LIMIT),
        name="win_attn",
    )(sink, proj, kc, vc, cos_t, sin_t)


NA_PAIRS = 2 * NA_KH - 2


def _na_bias_kernel(rpb_ref, o_ref):
    layer = pl.program_id(0)
    h = pl.program_id(1)
    n_dy = 2 * NA_KH - 1
    n_dx = 2 * NA_KW - 1
    base = (layer * D_HEADS + h) * (n_dy * n_dx)
    qc = lax.broadcasted_iota(jnp.int32, (GRID_W, 2 * GRID_W), 0)
    lane = lax.broadcasted_iota(jnp.int32, (GRID_W, 2 * GRID_W), 1)
    second = lane >= GRID_W
    kc = lane & (GRID_W - 1)
    idx = jnp.clip(kc - qc, -(NA_KW - 1), NA_KW - 1) + (NA_KW - 1)
    cs = jnp.clip(qc - NA_KW // 2, 0, GRID_W - NA_KW)
    col_ok = (kc >= cs) & (kc < cs + NA_KW)
    for e in range(NA_PAIRS):
        val = jnp.zeros((GRID_W, 2 * GRID_W), F32)
        for d in range(n_dx):
            r0 = rpb_ref[base + e * n_dx + d]
            r1 = rpb_ref[base + (e + 1) * n_dx + d]
            val = jnp.where(idx == d, jnp.where(second, r1, r0), val)
        o_ref[0, 0, e] = jnp.where(col_ok, val, NEG_INF)


def _na_bias(na_rpb):
    return pl.pallas_call(
        _na_bias_kernel,
        grid=(DEPTH, D_HEADS),
        in_specs=[pl.BlockSpec(memory_space=pltpu.SMEM)],
        out_specs=pl.BlockSpec((1, 1, NA_PAIRS, GRID_W, 2 * GRID_W), lambda l, h: (l, h, 0, 0, 0)),
        out_shape=jax.ShapeDtypeStruct((DEPTH, D_HEADS, NA_PAIRS, GRID_W, 2 * GRID_W), F32),
        name="na_bias",
    )(na_rpb.reshape(-1))


def _na_kernel(p_ref, kc_ref, vc_ref, tab_ref, o_ref, q_s, k_s, v_s):
    seq = DEC_SEQ
    n_rows = seq // GRID_W
    kh = min(NA_KH, n_rows)
    hw = D_HEADS * HEAD_DIM
    nb = seq // CHUNK

    def prep_body(c, carry):
        rows = pl.ds(pl.multiple_of(c * CHUNK, CHUNK), CHUNK)
        q_s[rows, :] = p_ref[rows, 0:hw].astype(BF16)
        k_s[rows, :] = p_ref[rows, hw:2 * hw].astype(BF16)
        v_s[rows, :] = p_ref[rows, 2 * hw:3 * hw].astype(BF16)
        return carry

    lax.fori_loop(0, nb, prep_body, 0)

    def row_body(r, carry):
        rs = jnp.clip(r - kh // 2, 0, n_rows - kh)
        e0 = rs - r + (NA_KH - 1)
        qrows = pl.ds(pl.multiple_of(r * GRID_W, GRID_W), GRID_W)
        krows = pl.ds(pl.multiple_of(rs * GRID_W, GRID_W), kh * GRID_W)
        for h in range(D_HEADS):
            cols = slice(h * HEAD_DIM, (h + 1) * HEAD_DIM)
            q = q_s[qrows, cols]
            bias = jnp.concatenate([tab_ref[0, h, e0 + 2 * i] for i in range(kh // 2)], axis=1)
            s_loc = lax.dot_general(q, k_s[krows, cols], TRANS_B, preferred_element_type=F32) * ATT_SCALE + bias
            s_ctx = lax.dot_general(q, kc_ref[0, 0, :, cols].astype(BF16), TRANS_B,
                                    preferred_element_type=F32) * ATT_SCALE
            o = _softmax_av([s_loc, s_ctx], [v_s[krows, cols], vc_ref[0, 0, :, cols].astype(BF16)])
            gate = p_ref[qrows, 3 * hw + h * HEAD_DIM:3 * hw + (h + 1) * HEAD_DIM]
            o_ref[qrows, cols] = (o * _silu(gate)).astype(BF16)
        return carry

    lax.fori_loop(0, n_rows, row_body, 0)


def _na_attn(proj, kc, vc, tab, layer):
    seq = DEC_SEQ
    hw = D_HEADS * HEAD_DIM
    cache_spec = pl.BlockSpec((1, 1, PAST_LEN, hw), lambda b: (b, layer, 0, 0))
    return pl.pallas_call(
        _na_kernel,
        grid=(DEC_BATCH,),
        in_specs=[
            pl.BlockSpec((seq, 4 * hw), lambda b: (b, 2)),
            cache_spec, cache_spec,
            pl.BlockSpec((1, D_HEADS, NA_PAIRS, GRID_W, 2 * GRID_W), lambda b: (layer, 0, 0, 0, 0)),
        ],
        out_specs=pl.BlockSpec((seq, hw), lambda b: (b, 0)),
        out_shape=jax.ShapeDtypeStruct((DEC_BATCH * seq, hw), BF16),
        scratch_shapes=[pltpu.VMEM((seq, hw), BF16)] * 3,
        compiler_params=pltpu.CompilerParams(vmem_limit_bytes=VMEM_LIMIT),
        name="na_attn",
    )(proj, kc, vc, tab)


def _rope_tables(seq):
    t = jnp.arange(seq)
    rows = (t // GRID_W).astype(F32)
    cols = (t % GRID_W).astype(F32)
    nf = HEAD_DIM // 4
    inv = ROPE_THETA ** (-jnp.arange(nf, dtype=F32) / nf)
    ar = rows[:, None] * inv[None, :]
    ac = cols[:, None] * inv[None, :]
    cos_t = jnp.concatenate([jnp.cos(ar), jnp.cos(ar), jnp.cos(ac), jnp.cos(ac)], axis=-1)
    sin_t = jnp.concatenate([-jnp.sin(ar), jnp.sin(ar), -jnp.sin(ac), jnp.sin(ac)], axis=-1)
    return cos_t, sin_t


def _pad_lanes(v, width=LANES):
    v = v.reshape(1, -1)
    return jnp.pad(v, ((0, 0), (0, width - v.shape[1])))


def kernel(x_prompt, x_sample, cache_attn_k, cache_attn_v, cache_na_k, cache_na_v, state_ssm_fwd, state_ssm_bwd, c, c_ctx, w_ada, b_ada, w_in, w_out, ln_g, ln_b, attn_sink, ssm_conv_w, ssm_conv_b, ssm_a_log, ssm_dt_bias, ssm_d, ssm_norm_w, pool_w, pool_b, pool_scale, na_rpb):
    alpha = (2.0 * DEPTH) ** 0.25
    n_x = B_HEADS * B_HEAD_DIM
    xc = x_prompt.reshape(BATCH * SEQ, D_MODEL)
    xl = x_sample.reshape(DEC_BATCH * DEC_SEQ, D_MODEL)

    cv = jnp.zeros((16, D_MODEL), F32).at[0].set(c_ctx).at[1:1 + DEC_BATCH].set(c)
    mod = _ada(cv, w_ada, b_ada)
    cos_t, sin_t = _rope_tables(DEC_SEQ)
    na_tab = _na_bias(na_rpb)

    kc_a = cache_attn_k.reshape(DEC_BATCH, DEPTH, PAST_LEN, A_KV_HEADS * HEAD_DIM)
    vc_a = cache_attn_v.reshape(DEC_BATCH, DEPTH, PAST_LEN, A_KV_HEADS * HEAD_DIM)
    kc_d = cache_na_k.reshape(DEC_BATCH, DEPTH, PAST_LEN, D_HEADS * HEAD_DIM)
    vc_d = cache_na_v.reshape(DEC_BATCH, DEPTH, PAST_LEN, D_HEADS * HEAD_DIM)
    h0f = state_ssm_fwd.reshape(DEC_BATCH, DEPTH, n_x, B_STATE)
    h0b = state_ssm_bwd.reshape(DEC_BATCH, DEPTH, n_x, B_STATE)

    ctx_row = lambda i: 0
    lat_in_row = lambda i: 1 + i * IN_TM // DEC_SEQ
    lat_out_row = lambda i: 1 + i * OUT_TM // DEC_SEQ

    states = []
    for l in range(DEPTH):
        w_l = w_in[l]
        w_main = jnp.concatenate([w_l[:, :DT_OFF], w_l[:, DT_OFF + DT_COLS:]], axis=1).astype(BF16)
        w_dt = jnp.pad(w_l[:, DT_OFF:DT_OFF + DT_COLS], ((0, 0), (0, LANES - DT_COLS))).astype(BF16)
        w_o = w_out[l].astype(BF16)
        mod3 = mod[l].reshape(16, 1, 3 * D_MODEL)
        lp = {
            "conv_w": jnp.pad(ssm_conv_w[l], ((0, ROW_PAD - B_CONV), (0, 0))),
            "conv_b": ssm_conv_b[l].reshape(1, -1),
            "dt_bias": _pad_lanes(ssm_dt_bias[l]),
            "a_log": _pad_lanes(ssm_a_log[l]),
            "d_skip": jnp.repeat(ssm_d[l], B_HEAD_DIM).reshape(1, n_x),
            "norm_w": ssm_norm_w[l].reshape(1, n_x),
            "pool_w": pool_w[l],
            "pool_b": pool_b[l].reshape(len(POOL_WINDOWS), 1, LANES),
            "pool_scale": pool_scale[l].reshape(1, D_BRANCH),
        }
        sink = attn_sink[l]

        proj_c, dt_c = _inproj(xc, mod3, w_main, w_dt, ctx_row)
        o_a = _ctx_attn(proj_c, sink, n_q=A_HEADS, n_kv=A_KV_HEADS, col_block=0, width=3 * D_BRANCH, use_sink=True)
        o_b, hf, hb = _ssm(proj_c, dt_c, lp, n_seq=BATCH, seq=SEQ)
        o_c = _pool(proj_c, lp, n_seq=BATCH, seq=SEQ)
        o_d = _ctx_attn(proj_c, sink, n_q=D_HEADS, n_kv=D_HEADS, col_block=2, width=4 * D_BRANCH, use_sink=False)
        xc = _outproj((o_a, o_b, o_c, o_d), xc, mod3, w_o, ln_g[l], ln_b[l], ctx_row, alpha)
        p3 = proj_c.reshape(BATCH, SEQ, D_MAIN)
        states.append((
            p3[:, :, 512:768].reshape(BATCH, SEQ, A_KV_HEADS, HEAD_DIM),
            p3[:, :, 768:1024].reshape(BATCH, SEQ, A_KV_HEADS, HEAD_DIM),
            p3[:, :, 4608:5120].reshape(BATCH, SEQ, D_HEADS, HEAD_DIM),
            p3[:, :, 5120:5632].reshape(BATCH, SEQ, D_HEADS, HEAD_DIM),
            hf.reshape(BATCH, B_HEADS, B_HEAD_DIM, B_STATE),
            hb.reshape(BATCH, B_HEADS, B_HEAD_DIM, B_STATE),
        ))

        proj_l, dt_l = _inproj(xl, mod3, w_main, w_dt, lat_in_row)
        o_a = _win_attn(proj_l, sink, kc_a, vc_a, cos_t, sin_t, l)
        o_b = _ssm(proj_l, dt_l, lp, n_seq=DEC_BATCH, seq=DEC_SEQ, h0f=h0f[:, l], h0b=h0b[:, l])[0]
        o_c = _pool(proj_l, lp, n_seq=DEC_BATCH, seq=DEC_SEQ)
        o_d = _na_attn(proj_l, kc_d, vc_d, na_tab, l)
        xl = _outproj((o_a, o_b, o_c, o_d), xl, mod3, w_o, ln_g[l], ln_b[l], lat_out_row, alpha)

    outs = tuple(jnp.stack([s[i] for s in states], axis=1) for i in range(6))
    return (xc.reshape(BATCH, SEQ, D_MODEL), xl.reshape(DEC_BATCH, DEC_SEQ, D_MODEL)) + outs
```

```python
import functools

import jax
import jax.numpy as jnp
from jax import lax
from jax.experimental import pallas as pl
from jax.experimental.pallas import tpu as pltpu

F32 = jnp.float32
BF16 = jnp.bfloat16

D_MODEL = 2048
BATCH = 32
SEQ = 256
DEPTH = 2
DEC_BATCH = 4
DEC_SEQ = 1024
PAST_LEN = 256
GRID_W = 64
D_BRANCH = 512
HEAD_DIM = 128
A_HEADS = 4
A_KV_HEADS = 2
A_WINDOW = 128
ROPE_THETA = 10000.0
B_HEADS = 8
B_HEAD_DIM = 64
B_STATE = 128
B_CONV = 5
CHUNK = 128
POOL_WINDOWS = (2, 4, 8, 16)
D_HEADS = 4
NA_KH = 8
NA_KW = 16
LN_EPS = 1e-6
NEG_INF = -1e30
ATT_SCALE = HEAD_DIM ** -0.5

D_MAIN = 6144
DT_OFF = 3072
DT_COLS = 16
LANES = 128
ROW_PAD = 8
VMEM_LIMIT = 48 * 1024 * 1024

TRANS_B = (((1,), (1,)), ((), ()))


def _silu(x):
    return x / (1.0 + jnp.exp(-x))


def _split_bf16(a, parts):
    out = []
    rem = a
    for _ in range(parts):
        hi = rem.astype(BF16)
        out.append(hi)
        rem = rem - hi.astype(F32)
    return out


def _dot01_lhs(m01, a, parts=3):
    acc = None
    for p in _split_bf16(a, parts):
        t = jnp.dot(m01, p, preferred_element_type=F32)
        acc = t if acc is None else acc + t
    return acc


def _dot01_rhs(a, m01, parts=2):
    acc = None
    for p in _split_bf16(a, parts):
        t = jnp.dot(p, m01, preferred_element_type=F32)
        acc = t if acc is None else acc + t
    return acc


def _softmax_av(scores, values, sink=None):
    m = None
    for s in scores:
        bm = jnp.max(s, axis=-1, keepdims=True)
        m = bm if m is None else jnp.maximum(m, bm)
    if sink is not None:
        m = jnp.maximum(m, sink)
    den = None
    acc = None
    for s, v in zip(scores, values):
        p = jnp.exp(s - m)
        d = jnp.sum(p, axis=-1, keepdims=True)
        o = jnp.dot(p.astype(BF16), v, preferred_element_type=F32)
        den = d if den is None else den + d
        acc = o if acc is None else acc + o
    if sink is not None:
        den = den + jnp.exp(sink - m)
    return acc / den


def _ada_kernel(cv_ref, w_ref, b_ref, o_ref):
    a = _silu(cv_ref[...]).astype(BF16)
    w = w_ref[0].astype(BF16)
    o_ref[0] = jnp.dot(a, w, preferred_element_type=F32) + b_ref[0]


def _ada(cv, w_ada, b_ada):
    tn = 1024
    n = 3 * D_MODEL
    return pl.pallas_call(
        _ada_kernel,
        grid=(DEPTH, n // tn),
        in_specs=[
            pl.BlockSpec((16, D_MODEL), lambda l, j: (0, 0)),
            pl.BlockSpec((1, D_MODEL, tn), lambda l, j: (l, 0, j)),
            pl.BlockSpec((1, 1, tn), lambda l, j: (l, 0, j)),
        ],
        out_specs=pl.BlockSpec((1, 16, tn), lambda l, j: (l, 0, j)),
        out_shape=jax.ShapeDtypeStruct((DEPTH, 16, n), F32),
        compiler_params=pltpu.CompilerParams(vmem_limit_bytes=VMEM_LIMIT),
        name="ada_mod",
    )(cv, w_ada, b_ada.reshape(DEPTH, 1, n))


PREP_TN = 512
PREP_RC = 256
N_LOW = DT_OFF // PREP_TN


def _prep_w_kernel(a_ref, b_ref, c_ref, o_ref, odt_ref):
    j = pl.program_id(1)

    @pl.when(j == 0)
    def _():
        lane = lax.broadcasted_iota(jnp.int32, (D_MODEL, LANES), 1)
        odt_ref[0] = jnp.where(lane < DT_COLS, c_ref[0], 0.0).astype(BF16)

    @pl.when(j < N_LOW)
    def _():
        o_ref[0] = a_ref[0].astype(BF16)

    @pl.when(j >= N_LOW)
    def _():
        def body(r, carry):
            rows = pl.ds(pl.multiple_of(r * PREP_RC, PREP_RC), PREP_RC)
            full = jnp.concatenate([a_ref[0, rows, :], b_ref[0, rows, :]], axis=1)
            o_ref[0, rows, :] = full[:, DT_COLS:DT_COLS + PREP_TN].astype(BF16)
            return carry

        lax.fori_loop(0, D_MODEL // PREP_RC, body, 0)


def _prep_w(w_in):
    blocks_per_tile = PREP_TN // LANES
    return pl.pallas_call(
        _prep_w_kernel,
        grid=(DEPTH, D_MAIN // PREP_TN),
        in_specs=[
            pl.BlockSpec((1, D_MODEL, PREP_TN), lambda l, j: (l, 0, j)),
            pl.BlockSpec((1, D_MODEL, LANES), lambda l, j: (l, 0, jnp.where(j >= N_LOW, (j + 1) * blocks_per_tile, 0))),
            pl.BlockSpec((1, D_MODEL, LANES), lambda l, j: (l, 0, DT_OFF // LANES)),
        ],
        out_specs=[
            pl.BlockSpec((1, D_MODEL, PREP_TN), lambda l, j: (l, 0, j)),
            pl.BlockSpec((1, D_MODEL, LANES), lambda l, j: (l, 0, 0)),
        ],
        out_shape=[
            jax.ShapeDtypeStruct((DEPTH, D_MODEL, D_MAIN), BF16),
            jax.ShapeDtypeStruct((DEPTH, D_MODEL, LANES), BF16),
        ],
        compiler_params=pltpu.CompilerParams(vmem_limit_bytes=VMEM_LIMIT),
        name="prep_w",
    )(w_in, w_in, w_in)


IN_TM = 1024
IN_TN = 1024
IN_RC = 128


def _inproj_kernel(x_ref, mod_ref, w_ref, wdt_ref, o_ref, dt_ref, u_ref):
    @pl.when(pl.program_id(1) == 0)
    def _():
        shift = mod_ref[0, :, 0:D_MODEL]
        scale1 = 1.0 + mod_ref[0, :, D_MODEL:2 * D_MODEL]

        def body(r, carry):
            rows = pl.ds(pl.multiple_of(r * IN_RC, IN_RC), IN_RC)
            xf = x_ref[rows, :]
            mu = jnp.mean(xf, axis=-1, keepdims=True)
            xc = xf - mu
            var = jnp.mean(xc * xc, axis=-1, keepdims=True)
            u = xc * lax.rsqrt(var + LN_EPS) * scale1 + shift
            u_ref[rows, :] = u.astype(BF16)
            return carry

        lax.fori_loop(0, IN_TM // IN_RC, body, 0)
        dt_ref[...] = jnp.dot(u_ref[...], wdt_ref[0], preferred_element_type=F32)

    o_ref[...] = jnp.dot(u_ref[...], w_ref[0], preferred_element_type=F32)


def _inproj(x2d, mod3, w_main, w_dt, mod_row, layer):
    m = x2d.shape[0]
    return pl.pallas_call(
        _inproj_kernel,
        grid=(m // IN_TM, D_MAIN // IN_TN),
        in_specs=[
            pl.BlockSpec((IN_TM, D_MODEL), lambda i, j: (i, 0)),
            pl.BlockSpec((1, 1, 3 * D_MODEL), lambda i, j: (mod_row(i), 0, 0)),
            pl.BlockSpec((1, D_MODEL, IN_TN), lambda i, j: (layer, 0, j)),
            pl.BlockSpec((1, D_MODEL, LANES), lambda i, j: (layer, 0, 0)),
        ],
        out_specs=[
            pl.BlockSpec((IN_TM, IN_TN), lambda i, j: (i, j)),
            pl.BlockSpec((IN_TM, LANES), lambda i, j: (i, 0)),
        ],
        out_shape=[
            jax.ShapeDtypeStruct((m, D_MAIN), F32),
            jax.ShapeDtypeStruct((m, LANES), F32),
        ],
        scratch_shapes=[pltpu.VMEM((IN_TM, D_MODEL), BF16)],
        compiler_params=pltpu.CompilerParams(
            dimension_semantics=("arbitrary", "arbitrary"), vmem_limit_bytes=VMEM_LIMIT),
        name="in_proj",
    )(x2d, mod3, w_main, w_dt)


OUT_TM = 512
OUT_RC = 128


def _outproj_kernel(ma_ref, mb_ref, mc_ref, md_ref, x_ref, mod_ref, w_ref, g_ref, b_ref, o_ref, acc_ref, *, alpha):
    gate = mod_ref[0, :, 2 * D_MODEL:3 * D_MODEL]
    mixed = jnp.concatenate([ma_ref[...], mb_ref[...], mc_ref[...], md_ref[...]], axis=1)
    acc_ref[...] = jnp.dot(mixed, w_ref[0], preferred_element_type=F32)

    def body(r, carry):
        rows = pl.ds(pl.multiple_of(r * OUT_RC, OUT_RC), OUT_RC)
        z = alpha * x_ref[rows, :] + acc_ref[rows, :] * gate
        mu = jnp.mean(z, axis=-1, keepdims=True)
        zc = z - mu
        var = jnp.mean(zc * zc, axis=-1, keepdims=True)
        o_ref[rows, :] = zc * lax.rsqrt(var + LN_EPS) * g_ref[...] + b_ref[...]
        return carry

    lax.fori_loop(0, OUT_TM // OUT_RC, body, 0)


def _outproj(mixed, x2d, mod3, w_o, ln_g, ln_b, mod_row, alpha, layer):
    m = x2d.shape[0]
    mspec = pl.BlockSpec((OUT_TM, D_BRANCH), lambda i: (i, 0))
    return pl.pallas_call(
        functools.partial(_outproj_kernel, alpha=alpha),
        grid=(m // OUT_TM,),
        in_specs=[
            mspec, mspec, mspec, mspec,
            pl.BlockSpec((OUT_TM, D_MODEL), lambda i: (i, 0)),
            pl.BlockSpec((1, 1, 3 * D_MODEL), lambda i: (mod_row(i), 0, 0)),
            pl.BlockSpec((1, D_MODEL, D_MODEL), lambda i: (layer, 0, 0)),
            pl.BlockSpec((1, D_MODEL), lambda i: (0, 0)),
            pl.BlockSpec((1, D_MODEL), lambda i: (0, 0)),
        ],
        out_specs=pl.BlockSpec((OUT_TM, D_MODEL), lambda i: (i, 0)),
        out_shape=jax.ShapeDtypeStruct((m, D_MODEL), F32),
        scratch_shapes=[pltpu.VMEM((OUT_TM, D_MODEL), F32)],
        compiler_params=pltpu.CompilerParams(vmem_limit_bytes=VMEM_LIMIT),
        name="out_proj",
    )(*mixed, x2d, mod3, w_o, ln_g.reshape(1, D_MODEL), ln_b.reshape(1, D_MODEL))


def _ctx_attn_kernel(*refs, n_q, n_kv, use_sink, aliased):
    sink_ref, p_ref = refs[0], refs[1]
    o_ref, kn_ref, vn_ref = refs[4:7] if aliased else refs[2:5]
    grp = n_q // n_kv
    k_off = n_q * HEAD_DIM
    v_off = k_off + n_kv * HEAD_DIM
    g_off = v_off + n_kv * HEAD_DIM
    for kk in range(n_kv):
        k32 = p_ref[:, k_off + kk * HEAD_DIM:k_off + (kk + 1) * HEAD_DIM]
        v32 = p_ref[:, v_off + kk * HEAD_DIM:v_off + (kk + 1) * HEAD_DIM]
        kn_ref[0, 0, pl.ds(kk, SEQ, stride=n_kv), :] = k32
        vn_ref[0, 0, pl.ds(kk, SEQ, stride=n_kv), :] = v32
        k = k32.astype(BF16)
        v = v32.astype(BF16)
        for g in range(grp):
            h = kk * grp + g
            cols = slice(h * HEAD_DIM, (h + 1) * HEAD_DIM)
            q = p_ref[:, cols].astype(BF16)
            s = lax.dot_general(q, k, TRANS_B, preferred_element_type=F32) * ATT_SCALE
            o = _softmax_av([s], [v], sink_ref[h] if use_sink else None)
            gate = p_ref[:, g_off + h * HEAD_DIM:g_off + (h + 1) * HEAD_DIM]
            o_ref[:, cols] = (o * _silu(gate)).astype(BF16)


def _ctx_attn(proj, sink, prev_kv, layer, *, n_q, n_kv, col_block, width, use_sink):
    aliased = prev_kv is not None
    kv_spec = pl.BlockSpec((1, 1, SEQ * n_kv, HEAD_DIM), lambda b: (b, layer, 0, 0))
    kv_shape = jax.ShapeDtypeStruct((BATCH, DEPTH, SEQ * n_kv, HEAD_DIM), F32)
    in_specs = [
        pl.BlockSpec(memory_space=pltpu.SMEM),
        pl.BlockSpec((SEQ, width), lambda b: (b, col_block)),
    ]
    args = [sink, proj]
    if aliased:
        in_specs += [pl.BlockSpec(memory_space=pl.ANY)] * 2
        args += list(prev_kv)
    return pl.pallas_call(
        functools.partial(_ctx_attn_kernel, n_q=n_q, n_kv=n_kv, use_sink=use_sink, aliased=aliased),
        grid=(BATCH,),
        in_specs=in_specs,
        out_specs=[pl.BlockSpec((SEQ, n_q * HEAD_DIM), lambda b: (b, 0)), kv_spec, kv_spec],
        out_shape=[jax.ShapeDtypeStruct((BATCH * SEQ, n_q * HEAD_DIM), BF16), kv_shape, kv_shape],
        input_output_aliases={2: 1, 3: 2} if aliased else {},
        compiler_params=pltpu.CompilerParams(vmem_limit_bytes=VMEM_LIMIT),
        name="ctx_attn",
    )(*args)


def _ssm_kernel(*refs, seq, has_h0, emit_state, aliased):
    nc = seq // CHUNK
    it = iter(refs)
    p_ref, dt_ref, cw_ref, cb_ref, dtb_ref, alog_ref, dskip_ref, nw_ref = (next(it) for _ in range(8))
    h0f_ref = next(it) if has_h0 else None
    h0b_ref = next(it) if has_h0 else None
    if aliased:
        next(it), next(it)
    o_ref = next(it)
    hf_ref = next(it) if emit_state else None
    hb_ref = next(it) if emit_state else None
    xpad, xbc_s, y_s, eb_s, sb_s, db_s, ht_s, ex_s = (next(it) for _ in range(8))

    n_x = B_HEADS * B_HEAD_DIM
    n_xbc = 2 * n_x

    @pl.when(pl.program_id(0) == 0)
    def _():
        r_i = lax.broadcasted_iota(jnp.int32, (LANES, 2 * n_x), 0)
        c_i = lax.broadcasted_iota(jnp.int32, (LANES, 2 * n_x), 1)
        ex_s[...] = jnp.where(r_i == lax.shift_right_logical(c_i, 6), 1.0, 0.0).astype(BF16)

    zero_rows = jnp.zeros((ROW_PAD, n_xbc), F32)
    xpad[0:ROW_PAD, :] = zero_rows
    xpad[seq + ROW_PAD:seq + 2 * ROW_PAD, :] = zero_rows

    def copy_body(c, carry):
        r0 = pl.multiple_of(c * CHUNK, CHUNK)
        xpad[pl.ds(pl.multiple_of(r0 + ROW_PAD, ROW_PAD), CHUNK), :] = p_ref[pl.ds(r0, CHUNK), 0:n_xbc]
        return carry

    lax.fori_loop(0, nc, copy_body, 0)

    def load_state(h_ref):
        for blk in range(4):
            cols = slice(blk * LANES, (blk + 1) * LANES)
            if h_ref is None:
                ht_s[:, cols] = jnp.zeros((B_STATE, LANES), F32)
            else:
                ht_s[:, cols] = h_ref[0, 0, cols, :].T

    def store_state(h_ref):
        for blk in range(4):
            cols = slice(blk * LANES, (blk + 1) * LANES)
            h_ref[0, 0, cols, :] = ht_s[:, cols].T

    load_state(h0f_ref)

    def chunk_body(c, carry):
        r0 = pl.multiple_of(c * CHUNK, CHUNK)
        rows = pl.ds(r0, CHUNK)
        win = xpad[pl.ds(r0, CHUNK + 2 * ROW_PAD), :]
        acc = jnp.broadcast_to(cb_ref[...], (CHUNK, n_xbc))
        for k in range(B_CONV):
            sh = (B_CONV // 2 - k) % (CHUNK + 2 * ROW_PAD)
            rolled = win if sh == 0 else pltpu.roll(win, sh, axis=0)
            acc = acc + cw_ref[k:k + 1, :] * rolled[ROW_PAD:ROW_PAD + CHUNK, :]
        xbc = _silu(acc)
        xbc_s[rows, :] = xbc
        xs = xbc[:, 0:n_x]
        bm = xbc[:, n_x:n_x + 2 * B_STATE]
        cm = xbc[:, n_x + 2 * B_STATE:n_xbc]

        dtr = dt_ref[rows, :] + dtb_ref[...]
        dt = jnp.maximum(dtr, 0.0) + jnp.log1p(jnp.exp(-jnp.abs(dtr)))
        a = dt * (-jnp.exp(alog_ref[...]))
        ii = lax.broadcasted_iota(jnp.int32, (CHUNK, CHUNK), 0)
        jj = lax.broadcasted_iota(jnp.int32, (CHUNK, CHUNK), 1)
        tril = jj <= ii
        triu = jj >= ii
        lc_f = _dot01_lhs(jnp.where(tril, 1.0, 0.0).astype(BF16), a)
        lc_b = _dot01_lhs(jnp.where(triu, 1.0, 0.0).astype(BF16), a)
        fwd_lane = jj < B_HEADS
        lc = jnp.where(fwd_lane, lc_f, lc_b)
        lend = jnp.where(fwd_lane[0:1, :], lc[CHUNK - 1:CHUNK, :], lc[0:1, :])
        toend = jnp.exp(lend - lc) * dt
        e_exp = _dot01_rhs(jnp.exp(lc), ex_s[...])
        w_exp = _dot01_rhs(toend, ex_s[...])
        d_exp = _dot01_rhs(jnp.broadcast_to(jnp.exp(lend), (ROW_PAD, LANES)), ex_s[...])
        lc_t = lc.T
        dt_t = dt.T
        eb_s[rows, :] = e_exp[:, n_x:2 * n_x]
        db_s[c] = d_exp[:, n_x:2 * n_x]

        lane_g = lax.broadcasted_iota(jnp.int32, (CHUNK, 2 * LANES), 1)
        for g in range(2):
            gcols = slice(g * 2 * LANES, (g + 1) * 2 * LANES)
            cg = cm[:, g * B_STATE:(g + 1) * B_STATE].astype(BF16)
            bg = bm[:, g * B_STATE:(g + 1) * B_STATE]
            cb = lax.dot_general(cg, bg.astype(BF16), TRANS_B, preferred_element_type=F32)
            bg_t = bg.T.astype(BF16)
            xg = xs[:, gcols]
            xblk = jnp.concatenate(
                [jnp.where((lane_g >= hh * B_HEAD_DIM) & (lane_g < (hh + 1) * B_HEAD_DIM), xg, 0.0).astype(BF16)
                 for hh in range(4)], axis=0)
            for d in range(2):
                mask = tril if d == 0 else triu
                ms = []
                for hh in range(4):
                    col = d * B_HEADS + g * 4 + hh
                    seg = lc[:, col:col + 1] - lc_t[col:col + 1, :]
                    dec = jnp.exp(jnp.where(mask, seg, -jnp.inf))
                    ms.append((cb * dec * dt_t[col:col + 1, :]).astype(BF16))
                y_in = jnp.dot(jnp.concatenate(ms, axis=1), xblk, preferred_element_type=F32)
                wg = w_exp[:, d * n_x + g * 2 * LANES:d * n_x + (g + 1) * 2 * LANES]
                st = jnp.dot(bg_t, (xg * wg).astype(BF16), preferred_element_type=F32)
                if d == 0:
                    h_prev = ht_s[:, gcols]
                    y_x = e_exp[:, gcols] * jnp.dot(cg, h_prev.astype(BF16), preferred_element_type=F32)
                    y_s[rows, gcols] = y_in + y_x
                    ht_s[:, gcols] = d_exp[0:1, gcols] * h_prev + st
                else:
                    y_s[rows, gcols] = y_s[rows, gcols] + y_in
                    sb_s[c, :, gcols] = st
        return carry

    lax.fori_loop(0, nc, chunk_body, 0)
    if emit_state:
        store_state(hf_ref)
    load_state(h0b_ref)

    def bwd_body(i, carry):
        c = nc - 1 - i
        r0 = pl.multiple_of(c * CHUNK, CHUNK)
        rows = pl.ds(r0, CHUNK)
        for g in range(2):
            gcols = slice(g * 2 * LANES, (g + 1) * 2 * LANES)
            cg = xbc_s[rows, n_x + 2 * B_STATE + g * B_STATE:n_x + 2 * B_STATE + (g + 1) * B_STATE].astype(BF16)
            h_prev = ht_s[:, gcols]
            y_x = eb_s[rows, gcols] * jnp.dot(cg, h_prev.astype(BF16), preferred_element_type=F32)
            y_s[rows, gcols] = y_s[rows, gcols] + y_x
            ht_s[:, gcols] = db_s[c, 0:1, gcols] * h_prev + sb_s[c, :, gcols]
        return carry

    lax.fori_loop(0, nc, bwd_body, 0)
    if emit_state:
        store_state(hb_ref)

    def out_body(c, carry):
        rows = pl.ds(pl.multiple_of(c * CHUNK, CHUNK), CHUNK)
        y = y_s[rows, :] + dskip_ref[...] * xbc_s[rows, 0:n_x]
        yz = y * _silu(p_ref[rows, n_xbc:n_xbc + n_x])
        ms = jnp.mean(yz * yz, axis=-1, keepdims=True)
        o_ref[rows, :] = (yz * lax.rsqrt(ms + LN_EPS) * nw_ref[...]).astype(BF16)
        return carry

    lax.fori_loop(0, nc, out_body, 0)


def _ssm(proj, dt, lp, layer, *, n_seq, seq, h0=None, prev_state=None):
    has_h0 = h0 is not None
    emit_state = not has_h0
    aliased = prev_state is not None
    nc = seq // CHUNK
    n_x = B_HEADS * B_HEAD_DIM
    const2 = lambda b: (0, 0)
    in_specs = [
        pl.BlockSpec((seq, 3 * n_x), lambda b: (b, 1)),
        pl.BlockSpec((seq, LANES), lambda b: (b, 0)),
        pl.BlockSpec((ROW_PAD, 2 * n_x), const2),
        pl.BlockSpec((1, 2 * n_x), const2),
        pl.BlockSpec((1, LANES), const2),
        pl.BlockSpec((1, LANES), const2),
        pl.BlockSpec((1, n_x), const2),
        pl.BlockSpec((1, n_x), const2),
    ]
    args = [proj, dt, lp["conv_w"], lp["conv_b"], lp["dt_bias"], lp["a_log"], lp["d_skip"], lp["norm_w"]]
    state_spec = pl.BlockSpec((1, 1, n_x, B_STATE), lambda b: (b, layer, 0, 0))
    if has_h0:
        in_specs += [state_spec, state_spec]
        args += list(h0)
    aliases = {}
    if aliased:
        aliases = {len(args): 1, len(args) + 1: 2}
        in_specs += [pl.BlockSpec(memory_space=pl.ANY)] * 2
        args += list(prev_state)
    out_specs = [pl.BlockSpec((seq, n_x), lambda b: (b, 0))]
    out_shape = [jax.ShapeDtypeStruct((n_seq * seq, n_x), BF16)]
    if emit_state:
        out_specs += [state_spec, state_spec]
        out_shape += [jax.ShapeDtypeStruct((n_seq, DEPTH, n_x, B_STATE), F32)] * 2
    return pl.pallas_call(
        functools.partial(_ssm_kernel, seq=seq, has_h0=has_h0, emit_state=emit_state, aliased=aliased),
        grid=(n_seq,),
        in_specs=in_specs,
        out_specs=out_specs,
        out_shape=out_shape,
        input_output_aliases=aliases,
        scratch_shapes=[
            pltpu.VMEM((seq + 2 * ROW_PAD, 2 * n_x), F32),
            pltpu.VMEM((seq, 2 * n_x), F32),
            pltpu.VMEM((seq, n_x), F32),
            pltpu.VMEM((seq, n_x), F32),
            pltpu.VMEM((nc, B_STATE, n_x), F32),
            pltpu.VMEM((nc, ROW_PAD, n_x), F32),
            pltpu.VMEM((B_STATE, n_x), F32),
            pltpu.VMEM((LANES, 2 * n_x), BF16),
        ],
        compiler_params=pltpu.CompilerParams(vmem_limit_bytes=VMEM_LIMIT),
        name="ssm",
    )(*args)


POOL_HALO = 16
POOL_WIN = CHUNK + 2 * POOL_HALO


def _pool_kernel(p_ref, w_ref, b_ref, sc_ref, o_ref, hi_s, lo_s, band_s, *, seq):
    nb = seq // CHUNK

    @pl.when(pl.program_id(0) == 0)
    def _():
        ii = lax.broadcasted_iota(jnp.int32, (CHUNK, POOL_WIN), 0)
        jj = lax.broadcasted_iota(jnp.int32, (CHUNK, POOL_WIN), 1)
        rel = jj - POOL_HALO - ii
        for g, w in enumerate(POOL_WINDOWS):
            band_s[g] = jnp.where((rel >= -(w // 2)) & (rel < w - w // 2), 1.0, 0.0).astype(BF16)

    zero_blk = jnp.zeros((POOL_HALO, D_BRANCH), BF16)
    for s in (hi_s, lo_s):
        s[0:POOL_HALO, :] = zero_blk
        s[seq + POOL_HALO:seq + 2 * POOL_HALO, :] = zero_blk

    def split_body(c, carry):
        r0 = pl.multiple_of(c * CHUNK, CHUNK)
        dst = pl.ds(pl.multiple_of(r0 + POOL_HALO, POOL_HALO), CHUNK)
        hi, lo = _split_bf16(p_ref[pl.ds(r0, CHUNK), 0:D_BRANCH], 2)
        hi_s[dst, :] = hi
        lo_s[dst, :] = lo
        return carry

    lax.fori_loop(0, nb, split_body, 0)

    def blk_body(rb, carry):
        r0 = pl.multiple_of(rb * CHUNK, CHUNK)
        rows = pl.ds(r0, CHUNK)
        win = pl.ds(r0, POOL_WIN)
        t = r0 + lax.broadcasted_iota(jnp.int32, (CHUNK, 1), 0)
        for g, w in enumerate(POOL_WINDOWS):
            cols = slice(g * LANES, (g + 1) * LANES)
            band = band_s[g]
            tot = (jnp.dot(band, hi_s[win, cols], preferred_element_type=F32)
                   + jnp.dot(band, lo_s[win, cols], preferred_element_type=F32))
            lo = jnp.clip(t - w // 2, 0, seq)
            hi = jnp.clip(t - w // 2 + w, 0, seq)
            pooled = tot / (hi - lo).astype(F32) - p_ref[rows, cols]
            out = jnp.dot(pooled.astype(BF16), w_ref[g].astype(BF16), preferred_element_type=F32) + b_ref[g]
            gate = p_ref[rows, D_BRANCH + g * LANES:D_BRANCH + (g + 1) * LANES]
            o_ref[rows, cols] = (out * sc_ref[:, cols] * _silu(gate)).astype(BF16)
        return carry

    lax.fori_loop(0, nb, blk_body, 0)


def _pool(proj, lp, *, n_seq, seq):
    return pl.pallas_call(
        functools.partial(_pool_kernel, seq=seq),
        grid=(n_seq,),
        in_specs=[
            pl.BlockSpec((seq, 2 * D_BRANCH), lambda b: (b, 3)),
            pl.BlockSpec((4, LANES, LANES), lambda b: (0, 0, 0)),
            pl.BlockSpec((4, 1, LANES), lambda b: (0, 0, 0)),
            pl.BlockSpec((1, D_BRANCH), lambda b: (0, 0)),
        ],
        out_specs=pl.BlockSpec((seq, D_BRANCH), lambda b: (b, 0)),
        out_shape=jax.ShapeDtypeStruct((n_seq * seq, D_BRANCH), BF16),
        scratch_shapes=[
            pltpu.VMEM((seq + 2 * POOL_HALO, D_BRANCH), BF16),
            pltpu.VMEM((seq + 2 * POOL_HALO, D_BRANCH), BF16),
            pltpu.VMEM((len(POOL_WINDOWS), CHUNK, POOL_WIN), BF16),
        ],
        compiler_params=pltpu.CompilerParams(vmem_limit_bytes=VMEM_LIMIT),
        name="pool",
    )(proj, lp["pool_w"], lp["pool_b"], lp["pool_scale"])


def _rope(x, cos, sin):
    lane = lax.broadcasted_iota(jnp.int32, x.shape, 1)
    first = (lane & 63) < 32
    swapped = jnp.where(first, pltpu.roll(x, 96, axis=1), pltpu.roll(x, 32, axis=1))
    return x * cos + swapped * sin


def _load_cache(c_ref, c_s, n_heads):
    for h in range(n_heads):
        c_s[:, h * HEAD_DIM:(h + 1) * HEAD_DIM] = c_ref[0, 0, pl.ds(h, PAST_LEN, stride=n_heads), :].astype(BF16)


def _win_attn_kernel(sink_ref, p_ref, kc_ref, vc_ref, cos_ref, sin_ref, o_ref, q_s, k_s, v_s, kc_s, vc_s):
    seq = DEC_SEQ
    nb = seq // CHUNK
    kv_w = A_KV_HEADS * HEAD_DIM
    k_off = A_HEADS * HEAD_DIM
    v_off = k_off + kv_w
    g_off = v_off + kv_w
    zero_blk = jnp.zeros((CHUNK, kv_w), BF16)
    for s in (k_s, v_s):
        s[0:CHUNK, :] = zero_blk
        s[seq + CHUNK:seq + 2 * CHUNK, :] = zero_blk
    _load_cache(kc_ref, kc_s, A_KV_HEADS)
    _load_cache(vc_ref, vc_s, A_KV_HEADS)

    def prep_body(c, carry):
        r0 = pl.multiple_of(c * CHUNK, CHUNK)
        rows = pl.ds(r0, CHUNK)
        prow = pl.ds(pl.multiple_of(r0 + CHUNK, CHUNK), CHUNK)
        cos = cos_ref[rows, :]
        sin = sin_ref[rows, :]
        for h in range(A_HEADS):
            cols = slice(h * HEAD_DIM, (h + 1) * HEAD_DIM)
            q_s[rows, cols] = _rope(p_ref[rows, cols], cos, sin).astype(BF16)
        for kk in range(A_KV_HEADS):
            cols = slice(kk * HEAD_DIM, (kk + 1) * HEAD_DIM)
            k_s[prow, cols] = _rope(p_ref[rows, k_off + kk * HEAD_DIM:k_off + (kk + 1) * HEAD_DIM], cos, sin).astype(BF16)
        v_s[prow, :] = p_ref[rows, v_off:v_off + kv_w].astype(BF16)
        return carry

    lax.fori_loop(0, nb, prep_body, 0)

    grp = A_HEADS // A_KV_HEADS

    def blk_body(n, carry):
        r0 = pl.multiple_of(n * CHUNK, CHUNK)
        rows = pl.ds(r0, CHUNK)
        win = pl.ds(r0, 3 * CHUNK)
        ii = lax.broadcasted_iota(jnp.int32, (grp * CHUNK, 3 * CHUNK), 0)
        jj = lax.broadcasted_iota(jnp.int32, (grp * CHUNK, 3 * CHUNK), 1)
        rel = jj - CHUNK - (ii & (CHUNK - 1))
        kpos = r0 - CHUNK + jj
        valid = (rel >= -A_WINDOW) & (rel <= A_WINDOW) & (kpos >= 0) & (kpos < seq)
        first_head = lax.broadcasted_iota(jnp.int32, (grp * CHUNK, 1), 0) < CHUNK
        for kk in range(A_KV_HEADS):
            kcols = slice(kk * HEAD_DIM, (kk + 1) * HEAD_DIM)
            h0 = kk * grp
            q = jnp.concatenate([q_s[rows, (h0 + g) * HEAD_DIM:(h0 + g + 1) * HEAD_DIM] for g in range(grp)], axis=0)
            s_loc = lax.dot_general(q, k_s[win, kcols], TRANS_B, preferred_element_type=F32) * ATT_SCALE
            s_loc = jnp.where(valid, s_loc, NEG_INF)
            s_ctx = lax.dot_general(q, kc_s[:, kcols], TRANS_B, preferred_element_type=F32) * ATT_SCALE
            sink = jnp.where(first_head, sink_ref[h0], sink_ref[h0 + 1])
            o = _softmax_av([s_loc, s_ctx], [v_s[win, kcols], vc_s[:, kcols]], sink)
            for g in range(grp):
                cols = slice((h0 + g) * HEAD_DIM, (h0 + g + 1) * HEAD_DIM)
                gate = p_ref[rows, g_off + (h0 + g) * HEAD_DIM:g_off + (h0 + g + 1) * HEAD_DIM]
                o_ref[rows, cols] = (o[g * CHUNK:(g + 1) * CHUNK, :] * _silu(gate)).astype(BF16)
        return carry

    lax.fori_loop(0, nb, blk_body, 0)


def _win_attn(proj, sink, kc, vc, cos_t, sin_t, layer):
    seq = DEC_SEQ
    kv_w = A_KV_HEADS * HEAD_DIM
    cache_spec = pl.BlockSpec((1, 1, PAST_LEN * A_KV_HEADS, HEAD_DIM), lambda b: (b, layer, 0, 0))
    tab_spec = pl.BlockSpec((seq, HEAD_DIM), lambda b: (0, 0))
    return pl.pallas_call(
        _win_attn_kernel,
        grid=(DEC_BATCH,),
        in_specs=[
            pl.BlockSpec(memory_space=pltpu.SMEM),
            pl.BlockSpec((seq, 3 * D_BRANCH), lambda b: (b, 0)),
            cache_spec, cache_spec, tab_spec, tab_spec,
        ],
        out_specs=pl.BlockSpec((seq, D_BRANCH), lambda b: (b, 0)),
        out_shape=jax.ShapeDtypeStruct((DEC_BATCH * seq, D_BRANCH), BF16),
        scratch_shapes=[
            pltpu.VMEM((seq, A_HEADS * HEAD_DIM), BF16),
            pltpu.VMEM((seq + 2 * CHUNK, kv_w), BF16),
            pltpu.VMEM((seq + 2 * CHUNK, kv_w), BF16),
            pltpu.VMEM((PAST_LEN, kv_w), BF16),
            pltpu.VMEM((PAST_LEN, kv_w), BF16),
        ],
        compiler_params=pltpu.CompilerParams(vmem_limit_bytes=VMEM_LIMIT),
        name="win_attn",
    )(sink, proj, kc, vc, cos_t, sin_t)


NA_PAIRS = 2 * NA_KH - 2


def _na_bias_kernel(rpb_ref, o_ref):
    layer = pl.program_id(0)
    h = pl.program_id(1)
    n_dy = 2 * NA_KH - 1
    n_dx = 2 * NA_KW - 1
    base = (layer * D_HEADS + h) * (n_dy * n_dx)
    qc = lax.broadcasted_iota(jnp.int32, (GRID_W, 2 * GRID_W), 0)
    lane = lax.broadcasted_iota(jnp.int32, (GRID_W, 2 * GRID_W), 1)
    second = lane >= GRID_W
    kc = lane & (GRID_W - 1)
    idx = jnp.clip(kc - qc, -(NA_KW - 1), NA_KW - 1) + (NA_KW - 1)
    cs = jnp.clip(qc - NA_KW // 2, 0, GRID_W - NA_KW)
    col_ok = (kc >= cs) & (kc < cs + NA_KW)
    for e in range(NA_PAIRS):
        val = jnp.zeros((GRID_W, 2 * GRID_W), F32)
        for d in range(n_dx):
            r0 = rpb_ref[base + e * n_dx + d]
            r1 = rpb_ref[base + (e + 1) * n_dx + d]
            val = jnp.where(idx == d, jnp.where(second, r1, r0), val)
        o_ref[0, 0, e] = jnp.where(col_ok, val, NEG_INF)


def _na_bias(na_rpb):
    return pl.pallas_call(
        _na_bias_kernel,
        grid=(DEPTH, D_HEADS),
        in_specs=[pl.BlockSpec(memory_space=pltpu.SMEM)],
        out_specs=pl.BlockSpec((1, 1, NA_PAIRS, GRID_W, 2 * GRID_W), lambda l, h: (l, h, 0, 0, 0)),
        out_shape=jax.ShapeDtypeStruct((DEPTH, D_HEADS, NA_PAIRS, GRID_W, 2 * GRID_W), F32),
        name="na_bias",
    )(na_rpb.reshape(-1))


NA_ROWS_PER_ITER = 2


def _na_kernel(p_ref, kc_ref, vc_ref, tab_ref, o_ref, q_s, k_s, v_s, kc_s, vc_s):
    seq = DEC_SEQ
    n_rows = seq // GRID_W
    kh = min(NA_KH, n_rows)
    hw = D_HEADS * HEAD_DIM
    nb = seq // CHUNK

    def prep_body(c, carry):
        rows = pl.ds(pl.multiple_of(c * CHUNK, CHUNK), CHUNK)
        q_s[rows, :] = p_ref[rows, 0:hw].astype(BF16)
        k_s[rows, :] = p_ref[rows, hw:2 * hw].astype(BF16)
        v_s[rows, :] = p_ref[rows, 2 * hw:3 * hw].astype(BF16)
        return carry

    lax.fori_loop(0, nb, prep_body, 0)
    _load_cache(kc_ref, kc_s, D_HEADS)
    _load_cache(vc_ref, vc_s, D_HEADS)

    def one_row(r):
        rs = jnp.clip(r - kh // 2, 0, n_rows - kh)
        e0 = rs - r + (NA_KH - 1)
        qrows = pl.ds(pl.multiple_of(r * GRID_W, GRID_W), GRID_W)
        krows = pl.ds(pl.multiple_of(rs * GRID_W, GRID_W), kh * GRID_W)
        for h in range(D_HEADS):
            cols = slice(h * HEAD_DIM, (h + 1) * HEAD_DIM)
            q = q_s[qrows, cols]
            bias = jnp.concatenate([tab_ref[0, h, e0 + 2 * i] for i in range(kh // 2)], axis=1)
            s_loc = lax.dot_general(q, k_s[krows, cols], TRANS_B, preferred_element_type=F32) * ATT_SCALE + bias
            s_ctx = lax.dot_general(q, kc_s[:, cols], TRANS_B, preferred_element_type=F32) * ATT_SCALE
            o = _softmax_av([s_loc, s_ctx], [v_s[krows, cols], vc_s[:, cols]])
            gate = p_ref[qrows, 3 * hw + h * HEAD_DIM:3 * hw + (h + 1) * HEAD_DIM]
            o_ref[qrows, cols] = (o * _silu(gate)).astype(BF16)

    def row_body(r2, carry):
        for dr in range(NA_ROWS_PER_ITER):
            one_row(r2 * NA_ROWS_PER_ITER + dr)
        return carry

    lax.fori_loop(0, n_rows // NA_ROWS_PER_ITER, row_body, 0)


def _na_attn(proj, kc, vc, tab, layer):
    seq = DEC_SEQ
    hw = D_HEADS * HEAD_DIM
    cache_spec = pl.BlockSpec((1, 1, PAST_LEN * D_HEADS, HEAD_DIM), lambda b: (b, layer, 0, 0))
    return pl.pallas_call(
        _na_kernel,
        grid=(DEC_BATCH,),
        in_specs=[
            pl.BlockSpec((seq, 4 * hw), lambda b: (b, 2)),
            cache_spec, cache_spec,
            pl.BlockSpec((1, D_HEADS, NA_PAIRS, GRID_W, 2 * GRID_W), lambda b: (layer, 0, 0, 0, 0)),
        ],
        out_specs=pl.BlockSpec((seq, hw), lambda b: (b, 0)),
        out_shape=jax.ShapeDtypeStruct((DEC_BATCH * seq, hw), BF16),
        scratch_shapes=[pltpu.VMEM((seq, hw), BF16)] * 3 + [pltpu.VMEM((PAST_LEN, hw), BF16)] * 2,
        compiler_params=pltpu.CompilerParams(vmem_limit_bytes=VMEM_LIMIT),
        name="na_attn",
    )(proj, kc, vc, tab)


def _rope_tables(seq):
    t = jnp.arange(seq)
    rows = (t // GRID_W).astype(F32)
    cols = (t % GRID_W).astype(F32)
    nf = HEAD_DIM // 4
    inv = ROPE_THETA ** (-jnp.arange(nf, dtype=F32) / nf)
    ar = rows[:, None] * inv[None, :]
    ac = cols[:, None] * inv[None, :]
    cos_t = jnp.concatenate([jnp.cos(ar), jnp.cos(ar), jnp.cos(ac), jnp.cos(ac)], axis=-1)
    sin_t = jnp.concatenate([-jnp.sin(ar), jnp.sin(ar), -jnp.sin(ac), jnp.sin(ac)], axis=-1)
    return cos_t, sin_t


def _pad_lanes(v, width=LANES):
    v = v.reshape(1, -1)
    return jnp.pad(v, ((0, 0), (0, width - v.shape[1])))


def kernel(x_prompt, x_sample, cache_attn_k, cache_attn_v, cache_na_k, cache_na_v, state_ssm_fwd, state_ssm_bwd, c, c_ctx, w_ada, b_ada, w_in, w_out, ln_g, ln_b, attn_sink, ssm_conv_w, ssm_conv_b, ssm_a_log, ssm_dt_bias, ssm_d, ssm_norm_w, pool_w, pool_b, pool_scale, na_rpb):
    alpha = (2.0 * DEPTH) ** 0.25
    n_x = B_HEADS * B_HEAD_DIM
    xc = x_prompt.reshape(BATCH * SEQ, D_MODEL)
    xl = x_sample.reshape(DEC_BATCH * DEC_SEQ, D_MODEL)

    cv = jnp.zeros((16, D_MODEL), F32).at[0].set(c_ctx).at[1:1 + DEC_BATCH].set(c)
    mod = _ada(cv, w_ada, b_ada)
    cos_t, sin_t = _rope_tables(DEC_SEQ)
    na_tab = _na_bias(na_rpb)

    kc_a = cache_attn_k.reshape(DEC_BATCH, DEPTH, PAST_LEN * A_KV_HEADS, HEAD_DIM)
    vc_a = cache_attn_v.reshape(DEC_BATCH, DEPTH, PAST_LEN * A_KV_HEADS, HEAD_DIM)
    kc_d = cache_na_k.reshape(DEC_BATCH, DEPTH, PAST_LEN * D_HEADS, HEAD_DIM)
    vc_d = cache_na_v.reshape(DEC_BATCH, DEPTH, PAST_LEN * D_HEADS, HEAD_DIM)
    h0 = (state_ssm_fwd.reshape(DEC_BATCH, DEPTH, n_x, B_STATE), state_ssm_bwd.reshape(DEC_BATCH, DEPTH, n_x, B_STATE))

    ctx_row = lambda i: 0
    lat_in_row = lambda i: 1 + i * IN_TM // DEC_SEQ
    lat_out_row = lambda i: 1 + i * OUT_TM // DEC_SEQ

    w_main, w_dt = _prep_w(w_in)
    w_o = w_out.astype(BF16)
    kv_a = kv_d = ssm_state = None
    for l in range(DEPTH):
        mod3 = mod[l].reshape(16, 1, 3 * D_MODEL)
        lp = {
            "conv_w": jnp.pad(ssm_conv_w[l], ((0, ROW_PAD - B_CONV), (0, 0))),
            "conv_b": ssm_conv_b[l].reshape(1, -1),
            "dt_bias": _pad_lanes(ssm_dt_bias[l]),
            "a_log": _pad_lanes(ssm_a_log[l]),
            "d_skip": jnp.repeat(ssm_d[l], B_HEAD_DIM).reshape(1, n_x),
            "norm_w": ssm_norm_w[l].reshape(1, n_x),
            "pool_w": pool_w[l],
            "pool_b": pool_b[l].reshape(len(POOL_WINDOWS), 1, LANES),
            "pool_scale": pool_scale[l].reshape(1, D_BRANCH),
        }
        sink = attn_sink[l]

        proj_c, dt_c = _inproj(xc, mod3, w_main, w_dt, ctx_row, l)
        o_a, *kv_a = _ctx_attn(proj_c, sink, kv_a, l, n_q=A_HEADS, n_kv=A_KV_HEADS, col_block=0,
                               width=3 * D_BRANCH, use_sink=True)
        o_b, *ssm_state = _ssm(proj_c, dt_c, lp, l, n_seq=BATCH, seq=SEQ, prev_state=ssm_state)
        o_c = _pool(proj_c, lp, n_seq=BATCH, seq=SEQ)
        o_d, *kv_d = _ctx_attn(proj_c, sink, kv_d, l, n_q=D_HEADS, n_kv=D_HEADS, col_block=2,
                               width=4 * D_BRANCH, use_sink=False)
        xc = _outproj((o_a, o_b, o_c, o_d), xc, mod3, w_o, ln_g[l], ln_b[l], ctx_row, alpha, l)

        proj_l, dt_l = _inproj(xl, mod3, w_main, w_dt, lat_in_row, l)
        o_a = _win_attn(proj_l, sink, kc_a, vc_a, cos_t, sin_t, l)
        o_b = _ssm(proj_l, dt_l, lp, l, n_seq=DEC_BATCH, seq=DEC_SEQ, h0=h0)[0]
        o_c = _pool(proj_l, lp, n_seq=DEC_BATCH, seq=DEC_SEQ)
        o_d = _na_attn(proj_l, kc_d, vc_d, na_tab, l)
        xl = _outproj((o_a, o_b, o_c, o_d), xl, mod3, w_o, ln_g[l], ln_b[l], lat_out_row, alpha, l)

    return (
        xc.reshape(BATCH, SEQ, D_MODEL),
        xl.reshape(DEC_BATCH, DEC_SEQ, D_MODEL),
        kv_a[0].reshape(BATCH, DEPTH, SEQ, A_KV_HEADS, HEAD_DIM),
        kv_a[1].reshape(BATCH, DEPTH, SEQ, A_KV_HEADS, HEAD_DIM),
        kv_d[0].reshape(BATCH, DEPTH, SEQ, D_HEADS, HEAD_DIM),
        kv_d[1].reshape(BATCH, DEPTH, SEQ, D_HEADS, HEAD_DIM),
        ssm_state[0].reshape(BATCH, DEPTH, B_HEADS, B_HEAD_DIM, B_STATE),
        ssm_state[1].reshape(BATCH, DEPTH, B_HEADS, B_HEAD_DIM, B_STATE),
    )
```

```python
import functools

import jax
import jax.numpy as jnp
from jax import lax
from jax.experimental import pallas as pl
from jax.experimental.pallas import tpu as pltpu

F32 = jnp.float32
BF16 = jnp.bfloat16

D_MODEL = 2048
BATCH = 32
SEQ = 256
DEPTH = 2
DEC_BATCH = 4
DEC_SEQ = 1024
PAST_LEN = 256
GRID_W = 64
D_BRANCH = 512
HEAD_DIM = 128
A_HEADS = 4
A_KV_HEADS = 2
A_WINDOW = 128
ROPE_THETA = 10000.0
B_HEADS = 8
B_HEAD_DIM = 64
B_STATE = 128
B_CONV = 5
CHUNK = 128
POOL_WINDOWS = (2, 4, 8, 16)
D_HEADS = 4
NA_KH = 8
NA_KW = 16
LN_EPS = 1e-6
NEG_INF = -1e30
ATT_SCALE = HEAD_DIM ** -0.5

D_MAIN = 6144
DT_OFF = 3072
DT_COLS = 16
LANES = 128
ROW_PAD = 8
VMEM_LIMIT = 48 * 1024 * 1024

TRANS_B = (((1,), (1,)), ((), ()))


def _silu(x):
    return x / (1.0 + jnp.exp(-x))


def _split_bf16(a, parts):
    out = []
    rem = a
    for _ in range(parts):
        hi = rem.astype(BF16)
        out.append(hi)
        rem = rem - hi.astype(F32)
    return out


def _dot01_lhs(m01, a, parts=3):
    acc = None
    for p in _split_bf16(a, parts):
        t = jnp.dot(m01, p, preferred_element_type=F32)
        acc = t if acc is None else acc + t
    return acc


def _dot01_rhs(a, m01, parts=2):
    acc = None
    for p in _split_bf16(a, parts):
        t = jnp.dot(p, m01, preferred_element_type=F32)
        acc = t if acc is None else acc + t
    return acc


def _softmax_av(scores, values, sink=None):
    m = None
    for s in scores:
        bm = jnp.max(s, axis=-1, keepdims=True)
        m = bm if m is None else jnp.maximum(m, bm)
    if sink is not None:
        m = jnp.maximum(m, sink)
    den = None
    acc = None
    for s, v in zip(scores, values):
        p = jnp.exp(s - m)
        d = jnp.sum(p, axis=-1, keepdims=True)
        o = jnp.dot(p.astype(BF16), v, preferred_element_type=F32)
        den = d if den is None else den + d
        acc = o if acc is None else acc + o
    if sink is not None:
        den = den + jnp.exp(sink - m)
    return acc / den


def _ada_kernel(cv_ref, w_ref, b_ref, o_ref):
    a = _silu(cv_ref[...]).astype(BF16)
    w = w_ref[0].astype(BF16)
    o_ref[0] = jnp.dot(a, w, preferred_element_type=F32) + b_ref[0]


def _ada(cv, w_ada, b_ada):
    tn = 1024
    n = 3 * D_MODEL
    return pl.pallas_call(
        _ada_kernel,
        grid=(DEPTH, n // tn),
        in_specs=[
            pl.BlockSpec((16, D_MODEL), lambda l, j: (0, 0)),
            pl.BlockSpec((1, D_MODEL, tn), lambda l, j: (l, 0, j)),
            pl.BlockSpec((1, 1, tn), lambda l, j: (l, 0, j)),
        ],
        out_specs=pl.BlockSpec((1, 16, tn), lambda l, j: (l, 0, j)),
        out_shape=jax.ShapeDtypeStruct((DEPTH, 16, n), F32),
        compiler_params=pltpu.CompilerParams(vmem_limit_bytes=VMEM_LIMIT),
        name="ada_mod",
    )(cv, w_ada, b_ada.reshape(DEPTH, 1, n))


PREP_TN = 512
PREP_KC = 256
N_LOW = DT_OFF // PREP_TN


def _prep_w_kernel(a_ref, b_ref, c_ref, o_ref, odt_ref):
    j = pl.program_id(1)
    kcs = [slice(k * PREP_KC, (k + 1) * PREP_KC) for k in range(D_MODEL // PREP_KC)]

    @pl.when(j == 0)
    def _():
        row = lax.broadcasted_iota(jnp.int32, (LANES, PREP_KC), 0)
        for ks in kcs:
            odt_ref[0, ks, :] = jnp.where(row < DT_COLS, c_ref[0, :, ks], 0.0).T.astype(BF16)

    @pl.when(j < N_LOW)
    def _():
        for ks in kcs:
            o_ref[0, ks, :] = a_ref[0, :, ks].T.astype(BF16)

    @pl.when(j >= N_LOW)
    def _():
        for ks in kcs:
            src = jnp.concatenate([a_ref[0, DT_COLS:, ks], b_ref[0, :, ks]], axis=0)
            o_ref[0, ks, :] = src.T.astype(BF16)


def _prep_w(w_in):
    w_t = jnp.swapaxes(w_in, 1, 2)
    tail_blocks = PREP_TN // DT_COLS
    return pl.pallas_call(
        _prep_w_kernel,
        grid=(DEPTH, D_MAIN // PREP_TN),
        in_specs=[
            pl.BlockSpec((1, PREP_TN, D_MODEL), lambda l, j: (l, j, 0)),
            pl.BlockSpec((1, DT_COLS, D_MODEL), lambda l, j: (l, jnp.where(j >= N_LOW, (j + 1) * tail_blocks, 0), 0)),
            pl.BlockSpec((1, LANES, D_MODEL), lambda l, j: (l, DT_OFF // LANES, 0)),
        ],
        out_specs=[
            pl.BlockSpec((1, D_MODEL, PREP_TN), lambda l, j: (l, 0, j)),
            pl.BlockSpec((1, D_MODEL, LANES), lambda l, j: (l, 0, 0)),
        ],
        out_shape=[
            jax.ShapeDtypeStruct((DEPTH, D_MODEL, D_MAIN), BF16),
            jax.ShapeDtypeStruct((DEPTH, D_MODEL, LANES), BF16),
        ],
        compiler_params=pltpu.CompilerParams(vmem_limit_bytes=VMEM_LIMIT),
        name="prep_w",
    )(w_t, w_t, w_t)


IN_TM = 1024
IN_TN = 1024
IN_RC = 128


def _inproj_kernel(x_ref, mod_ref, w_ref, wdt_ref, o_ref, dt_ref, u_ref):
    @pl.when(pl.program_id(1) == 0)
    def _():
        shift = mod_ref[0, :, 0:D_MODEL]
        scale1 = 1.0 + mod_ref[0, :, D_MODEL:2 * D_MODEL]

        def body(r, carry):
            rows = pl.ds(pl.multiple_of(r * IN_RC, IN_RC), IN_RC)
            xf = x_ref[rows, :]
            mu = jnp.mean(xf, axis=-1, keepdims=True)
            xc = xf - mu
            var = jnp.mean(xc * xc, axis=-1, keepdims=True)
            u = xc * lax.rsqrt(var + LN_EPS) * scale1 + shift
            u_ref[rows, :] = u.astype(BF16)
            return carry

        lax.fori_loop(0, IN_TM // IN_RC, body, 0)
        dt_ref[...] = jnp.dot(u_ref[...], wdt_ref[0], preferred_element_type=F32)

    o_ref[...] = jnp.dot(u_ref[...], w_ref[0], preferred_element_type=F32)


def _inproj(x2d, mod3, w_main, w_dt, mod_row, layer):
    m = x2d.shape[0]
    return pl.pallas_call(
        _inproj_kernel,
        grid=(m // IN_TM, D_MAIN // IN_TN),
        in_specs=[
            pl.BlockSpec((IN_TM, D_MODEL), lambda i, j: (i, 0)),
            pl.BlockSpec((1, 1, 3 * D_MODEL), lambda i, j: (mod_row(i), 0, 0)),
            pl.BlockSpec((1, D_MODEL, IN_TN), lambda i, j: (layer, 0, j)),
            pl.BlockSpec((1, D_MODEL, LANES), lambda i, j: (layer, 0, 0)),
        ],
        out_specs=[
            pl.BlockSpec((IN_TM, IN_TN), lambda i, j: (i, j)),
            pl.BlockSpec((IN_TM, LANES), lambda i, j: (i, 0)),
        ],
        out_shape=[
            jax.ShapeDtypeStruct((m, D_MAIN), F32),
            jax.ShapeDtypeStruct((m, LANES), F32),
        ],
        scratch_shapes=[pltpu.VMEM((IN_TM, D_MODEL), BF16)],
        compiler_params=pltpu.CompilerParams(
            dimension_semantics=("arbitrary", "arbitrary"), vmem_limit_bytes=VMEM_LIMIT),
        name="in_proj",
    )(x2d, mod3, w_main, w_dt)


OUT_TM = 512
OUT_RC = 128


def _outproj_kernel(ma_ref, mb_ref, mc_ref, md_ref, x_ref, mod_ref, w_ref, g_ref, b_ref, o_ref, acc_ref, *, alpha):
    gate = mod_ref[0, :, 2 * D_MODEL:3 * D_MODEL]
    mixed = jnp.concatenate([ma_ref[...], mb_ref[...], mc_ref[...], md_ref[...]], axis=1)
    acc_ref[...] = jnp.dot(mixed, w_ref[0], preferred_element_type=F32)

    def body(r, carry):
        rows = pl.ds(pl.multiple_of(r * OUT_RC, OUT_RC), OUT_RC)
        z = alpha * x_ref[rows, :] + acc_ref[rows, :] * gate
        mu = jnp.mean(z, axis=-1, keepdims=True)
        zc = z - mu
        var = jnp.mean(zc * zc, axis=-1, keepdims=True)
        o_ref[rows, :] = zc * lax.rsqrt(var + LN_EPS) * g_ref[...] + b_ref[...]
        return carry

    lax.fori_loop(0, OUT_TM // OUT_RC, body, 0)


def _outproj(mixed, x2d, mod3, w_o, ln_g, ln_b, mod_row, alpha, layer):
    m = x2d.shape[0]
    mspec = pl.BlockSpec((OUT_TM, D_BRANCH), lambda i: (i, 0))
    return pl.pallas_call(
        functools.partial(_outproj_kernel, alpha=alpha),
        grid=(m // OUT_TM,),
        in_specs=[
            mspec, mspec, mspec, mspec,
            pl.BlockSpec((OUT_TM, D_MODEL), lambda i: (i, 0)),
            pl.BlockSpec((1, 1, 3 * D_MODEL), lambda i: (mod_row(i), 0, 0)),
            pl.BlockSpec((1, D_MODEL, D_MODEL), lambda i: (layer, 0, 0)),
            pl.BlockSpec((1, D_MODEL), lambda i: (0, 0)),
            pl.BlockSpec((1, D_MODEL), lambda i: (0, 0)),
        ],
        out_specs=pl.BlockSpec((OUT_TM, D_MODEL), lambda i: (i, 0)),
        out_shape=jax.ShapeDtypeStruct((m, D_MODEL), F32),
        scratch_shapes=[pltpu.VMEM((OUT_TM, D_MODEL), F32)],
        compiler_params=pltpu.CompilerParams(vmem_limit_bytes=VMEM_LIMIT),
        name="out_proj",
    )(*mixed, x2d, mod3, w_o, ln_g.reshape(1, D_MODEL), ln_b.reshape(1, D_MODEL))


CTX_NS = 2


def _ctx_attn_kernel(*refs, n_q, n_kv, use_sink, aliased):
    sink_ref, p_ref = refs[0], refs[1]
    o_ref, kn_ref, vn_ref = refs[4:7] if aliased else refs[2:5]
    grp = n_q // n_kv
    k_off = n_q * HEAD_DIM
    v_off = k_off + n_kv * HEAD_DIM
    g_off = v_off + n_kv * HEAD_DIM
    for s_i in range(CTX_NS):
        rows = slice(s_i * SEQ, (s_i + 1) * SEQ)
        for kk in range(n_kv):
            k32 = p_ref[rows, k_off + kk * HEAD_DIM:k_off + (kk + 1) * HEAD_DIM]
            v32 = p_ref[rows, v_off + kk * HEAD_DIM:v_off + (kk + 1) * HEAD_DIM]
            kn_ref[s_i, 0, pl.ds(kk, SEQ, stride=n_kv), :] = k32
            vn_ref[s_i, 0, pl.ds(kk, SEQ, stride=n_kv), :] = v32
            k = k32.astype(BF16)
            v = v32.astype(BF16)
            for g in range(grp):
                h = kk * grp + g
                cols = slice(h * HEAD_DIM, (h + 1) * HEAD_DIM)
                q = p_ref[rows, cols].astype(BF16)
                s = lax.dot_general(q, k, TRANS_B, preferred_element_type=F32) * ATT_SCALE
                o = _softmax_av([s], [v], sink_ref[h] if use_sink else None)
                gate = p_ref[rows, g_off + h * HEAD_DIM:g_off + (h + 1) * HEAD_DIM]
                o_ref[rows, cols] = (o * _silu(gate)).astype(BF16)


def _ctx_attn(proj, sink, prev_kv, layer, *, n_q, n_kv, col_block, width, use_sink):
    aliased = prev_kv is not None
    kv_spec = pl.BlockSpec((CTX_NS, 1, SEQ * n_kv, HEAD_DIM), lambda b: (b, layer, 0, 0))
    kv_shape = jax.ShapeDtypeStruct((BATCH, DEPTH, SEQ * n_kv, HEAD_DIM), F32)
    in_specs = [
        pl.BlockSpec(memory_space=pltpu.SMEM),
        pl.BlockSpec((CTX_NS * SEQ, width), lambda b: (b, col_block)),
    ]
    args = [sink, proj]
    if aliased:
        in_specs += [pl.BlockSpec(memory_space=pl.ANY)] * 2
        args += list(prev_kv)
    return pl.pallas_call(
        functools.partial(_ctx_attn_kernel, n_q=n_q, n_kv=n_kv, use_sink=use_sink, aliased=aliased),
        grid=(BATCH // CTX_NS,),
        in_specs=in_specs,
        out_specs=[pl.BlockSpec((CTX_NS * SEQ, n_q * HEAD_DIM), lambda b: (b, 0)), kv_spec, kv_spec],
        out_shape=[jax.ShapeDtypeStruct((BATCH * SEQ, n_q * HEAD_DIM), BF16), kv_shape, kv_shape],
        input_output_aliases={2: 1, 3: 2} if aliased else {},
        compiler_params=pltpu.CompilerParams(vmem_limit_bytes=VMEM_LIMIT),
        name="ctx_attn",
    )(*args)


def _ssm_kernel(*refs, seq, has_h0, emit_state, aliased):
    nc = seq // CHUNK
    it = iter(refs)
    p_ref, dt_ref, cw_ref, cb_ref, dtb_ref, alog_ref, dskip_ref, nw_ref = (next(it) for _ in range(8))
    h0f_ref = next(it) if has_h0 else None
    h0b_ref = next(it) if has_h0 else None
    if aliased:
        next(it), next(it)
    o_ref = next(it)
    hf_ref = next(it) if emit_state else None
    hb_ref = next(it) if emit_state else None
    xpad, xbc_s, y_s, eb_s, sb_s, db_s, ht_s, ex_s = (next(it) for _ in range(8))

    n_x = B_HEADS * B_HEAD_DIM
    n_xbc = 2 * n_x

    @pl.when(pl.program_id(0) == 0)
    def _():
        r_i = lax.broadcasted_iota(jnp.int32, (LANES, 2 * n_x), 0)
        c_i = lax.broadcasted_iota(jnp.int32, (LANES, 2 * n_x), 1)
        ex_s[...] = jnp.where(r_i == lax.shift_right_logical(c_i, 6), 1.0, 0.0).astype(BF16)

    zero_rows = jnp.zeros((ROW_PAD, n_xbc), F32)
    xpad[0:ROW_PAD, :] = zero_rows
    xpad[seq + ROW_PAD:seq + 2 * ROW_PAD, :] = zero_rows

    def copy_body(c, carry):
        r0 = pl.multiple_of(c * CHUNK, CHUNK)
        xpad[pl.ds(pl.multiple_of(r0 + ROW_PAD, ROW_PAD), CHUNK), :] = p_ref[pl.ds(r0, CHUNK), 0:n_xbc]
        return carry

    lax.fori_loop(0, nc, copy_body, 0)

    def load_state(h_ref):
        for blk in range(4):
            cols = slice(blk * LANES, (blk + 1) * LANES)
            if h_ref is None:
                ht_s[:, cols] = jnp.zeros((B_STATE, LANES), F32)
            else:
                ht_s[:, cols] = h_ref[0, 0, cols, :].T

    def store_state(h_ref):
        for blk in range(4):
            cols = slice(blk * LANES, (blk + 1) * LANES)
            h_ref[0, 0, cols, :] = ht_s[:, cols].T

    load_state(h0f_ref)

    def chunk_body(c, carry):
        r0 = pl.multiple_of(c * CHUNK, CHUNK)
        rows = pl.ds(r0, CHUNK)
        win = xpad[pl.ds(r0, CHUNK + 2 * ROW_PAD), :]
        acc = jnp.broadcast_to(cb_ref[...], (CHUNK, n_xbc))
        for k in range(B_CONV):
            sh = (B_CONV // 2 - k) % (CHUNK + 2 * ROW_PAD)
            rolled = win if sh == 0 else pltpu.roll(win, sh, axis=0)
            acc = acc + cw_ref[k:k + 1, :] * rolled[ROW_PAD:ROW_PAD + CHUNK, :]
        xbc = _silu(acc)
        xbc_s[rows, :] = xbc
        xs = xbc[:, 0:n_x]
        bm = xbc[:, n_x:n_x + 2 * B_STATE]
        cm = xbc[:, n_x + 2 * B_STATE:n_xbc]

        dtr = dt_ref[rows, :] + dtb_ref[...]
        dt = jnp.maximum(dtr, 0.0) + jnp.log1p(jnp.exp(-jnp.abs(dtr)))
        a = dt * (-jnp.exp(alog_ref[...]))
        ii = lax.broadcasted_iota(jnp.int32, (CHUNK, CHUNK), 0)
        jj = lax.broadcasted_iota(jnp.int32, (CHUNK, CHUNK), 1)
        tril = jj <= ii
        triu = jj >= ii
        lc_f = _dot01_lhs(jnp.where(tril, 1.0, 0.0).astype(BF16), a)
        lc_b = _dot01_lhs(jnp.where(triu, 1.0, 0.0).astype(BF16), a)
        fwd_lane = jj < B_HEADS
        lc = jnp.where(fwd_lane, lc_f, lc_b)
        lend = jnp.where(fwd_lane[0:1, :], lc[CHUNK - 1:CHUNK, :], lc[0:1, :])
        toend = jnp.exp(lend - lc) * dt
        e_exp = _dot01_rhs(jnp.exp(lc), ex_s[...])
        w_exp = _dot01_rhs(toend, ex_s[...])
        d_exp = _dot01_rhs(jnp.broadcast_to(jnp.exp(lend), (ROW_PAD, LANES)), ex_s[...])
        lc_t = lc.T
        dt_t = dt.T
        eb_s[rows, :] = e_exp[:, n_x:2 * n_x]
        db_s[c] = d_exp[:, n_x:2 * n_x]

        lane_g = lax.broadcasted_iota(jnp.int32, (CHUNK, 2 * LANES), 1)
        for g in range(2):
            gcols = slice(g * 2 * LANES, (g + 1) * 2 * LANES)
            cg = cm[:, g * B_STATE:(g + 1) * B_STATE].astype(BF16)
            bg = bm[:, g * B_STATE:(g + 1) * B_STATE]
            cb = lax.dot_general(cg, bg.astype(BF16), TRANS_B, preferred_element_type=F32)
            bg_t = bg.T.astype(BF16)
            xg = xs[:, gcols]
            xblk = jnp.concatenate(
                [jnp.where((lane_g >= hh * B_HEAD_DIM) & (lane_g < (hh + 1) * B_HEAD_DIM), xg, 0.0).astype(BF16)
                 for hh in range(4)], axis=0)
            for d in range(2):
                mask = tril if d == 0 else triu
                ms = []
                for hh in range(4):
                    col = d * B_HEADS + g * 4 + hh
                    seg = lc[:, col:col + 1] - lc_t[col:col + 1, :]
                    dec = jnp.exp(jnp.where(mask, seg, -jnp.inf))
                    ms.append((cb * dec * dt_t[col:col + 1, :]).astype(BF16))
                y_in = jnp.dot(jnp.concatenate(ms, axis=1), xblk, preferred_element_type=F32)
                wg = w_exp[:, d * n_x + g * 2 * LANES:d * n_x + (g + 1) * 2 * LANES]
                st = jnp.dot(bg_t, (xg * wg).astype(BF16), preferred_element_type=F32)
                if d == 0:
                    h_prev = ht_s[:, gcols]
                    y_x = e_exp[:, gcols] * jnp.dot(cg, h_prev.astype(BF16), preferred_element_type=F32)
                    y_s[rows, gcols] = y_in + y_x
                    ht_s[:, gcols] = d_exp[0:1, gcols] * h_prev + st
                else:
                    y_s[rows, gcols] = y_s[rows, gcols] + y_in
                    sb_s[c, :, gcols] = st
        return carry

    lax.fori_loop(0, nc, chunk_body, 0)
    if emit_state:
        store_state(hf_ref)
    load_state(h0b_ref)

    def bwd_body(i, carry):
        c = nc - 1 - i
        r0 = pl.multiple_of(c * CHUNK, CHUNK)
        rows = pl.ds(r0, CHUNK)
        for g in range(2):
            gcols = slice(g * 2 * LANES, (g + 1) * 2 * LANES)
            cg = xbc_s[rows, n_x + 2 * B_STATE + g * B_STATE:n_x + 2 * B_STATE + (g + 1) * B_STATE].astype(BF16)
            h_prev = ht_s[:, gcols]
            y_x = eb_s[rows, gcols] * jnp.dot(cg, h_prev.astype(BF16), preferred_element_type=F32)
            y_s[rows, gcols] = y_s[rows, gcols] + y_x
            ht_s[:, gcols] = db_s[c, 0:1, gcols] * h_prev + sb_s[c, :, gcols]
        return carry

    lax.fori_loop(0, nc, bwd_body, 0)
    if emit_state:
        store_state(hb_ref)

    def out_body(c, carry):
        rows = pl.ds(pl.multiple_of(c * CHUNK, CHUNK), CHUNK)
        y = y_s[rows, :] + dskip_ref[...] * xbc_s[rows, 0:n_x]
        yz = y * _silu(p_ref[rows, n_xbc:n_xbc + n_x])
        ms = jnp.mean(yz * yz, axis=-1, keepdims=True)
        o_ref[rows, :] = (yz * lax.rsqrt(ms + LN_EPS) * nw_ref[...]).astype(BF16)
        return carry

    lax.fori_loop(0, nc, out_body, 0)


def _ssm(proj, dt, lp, layer, *, n_seq, seq, h0=None, prev_state=None):
    has_h0 = h0 is not None
    emit_state = not has_h0
    aliased = prev_state is not None
    nc = seq // CHUNK
    n_x = B_HEADS * B_HEAD_DIM
    const2 = lambda b: (0, 0)
    in_specs = [
        pl.BlockSpec((seq, 3 * n_x), lambda b: (b, 1)),
        pl.BlockSpec((seq, LANES), lambda b: (b, 0)),
        pl.BlockSpec((ROW_PAD, 2 * n_x), const2),
        pl.BlockSpec((1, 2 * n_x), const2),
        pl.BlockSpec((1, LANES), const2),
        pl.BlockSpec((1, LANES), const2),
        pl.BlockSpec((1, n_x), const2),
        pl.BlockSpec((1, n_x), const2),
    ]
    args = [proj, dt, lp["conv_w"], lp["conv_b"], lp["dt_bias"], lp["a_log"], lp["d_skip"], lp["norm_w"]]
    state_spec = pl.BlockSpec((1, 1, n_x, B_STATE), lambda b: (b, layer, 0, 0))
    if has_h0:
        in_specs += [state_spec, state_spec]
        args += list(h0)
    aliases = {}
    if aliased:
        aliases = {len(args): 1, len(args) + 1: 2}
        in_specs += [pl.BlockSpec(memory_space=pl.ANY)] * 2
        args += list(prev_state)
    out_specs = [pl.BlockSpec((seq, n_x), lambda b: (b, 0))]
    out_shape = [jax.ShapeDtypeStruct((n_seq * seq, n_x), BF16)]
    if emit_state:
        out_specs += [state_spec, state_spec]
        out_shape += [jax.ShapeDtypeStruct((n_seq, DEPTH, n_x, B_STATE), F32)] * 2
    return pl.pallas_call(
        functools.partial(_ssm_kernel, seq=seq, has_h0=has_h0, emit_state=emit_state, aliased=aliased),
        grid=(n_seq,),
        in_specs=in_specs,
        out_specs=out_specs,
        out_shape=out_shape,
        input_output_aliases=aliases,
        scratch_shapes=[
            pltpu.VMEM((seq + 2 * ROW_PAD, 2 * n_x), F32),
            pltpu.VMEM((seq, 2 * n_x), F32),
            pltpu.VMEM((seq, n_x), F32),
            pltpu.VMEM((seq, n_x), F32),
            pltpu.VMEM((nc, B_STATE, n_x), F32),
            pltpu.VMEM((nc, ROW_PAD, n_x), F32),
            pltpu.VMEM((B_STATE, n_x), F32),
            pltpu.VMEM((LANES, 2 * n_x), BF16),
        ],
        compiler_params=pltpu.CompilerParams(vmem_limit_bytes=VMEM_LIMIT),
        name="ssm",
    )(*args)


POOL_HALO = 16
POOL_WIN = CHUNK + 2 * POOL_HALO
POOL_UNROLL = 2


def _pool_kernel(p_ref, w_ref, b_ref, sc_ref, o_ref, hi_s, lo_s, band_s, *, seq, ns):
    nb = seq // CHUNK
    slot = seq + 2 * POOL_HALO

    def locate(c):
        s_i, cb = (0, c) if ns == 1 else (c // nb, c % nb)
        r0 = pl.multiple_of(c * CHUNK, CHUNK)
        return r0, pl.multiple_of(s_i * slot + cb * CHUNK, POOL_HALO), cb * CHUNK

    @pl.when(pl.program_id(0) == 0)
    def _():
        ii = lax.broadcasted_iota(jnp.int32, (CHUNK, POOL_WIN), 0)
        jj = lax.broadcasted_iota(jnp.int32, (CHUNK, POOL_WIN), 1)
        rel = jj - POOL_HALO - ii
        for g, w in enumerate(POOL_WINDOWS):
            band_s[g] = jnp.where((rel >= -(w // 2)) & (rel < w - w // 2), 1.0, 0.0).astype(BF16)

    zero_blk = jnp.zeros((POOL_HALO, D_BRANCH), BF16)
    for s in (hi_s, lo_s):
        for s_i in range(ns):
            s[s_i * slot:s_i * slot + POOL_HALO, :] = zero_blk
            s[(s_i + 1) * slot - POOL_HALO:(s_i + 1) * slot, :] = zero_blk

    def split_body(c, carry):
        r0, w0, _ = locate(c)
        dst = pl.ds(pl.multiple_of(w0 + POOL_HALO, POOL_HALO), CHUNK)
        hi, lo = _split_bf16(p_ref[pl.ds(r0, CHUNK), 0:D_BRANCH], 2)
        hi_s[dst, :] = hi
        lo_s[dst, :] = lo
        return carry

    lax.fori_loop(0, ns * nb, split_body, 0)

    def one_chunk(c):
        r0, w0, t0 = locate(c)
        rows = pl.ds(r0, CHUNK)
        win = pl.ds(w0, POOL_WIN)
        t = t0 + lax.broadcasted_iota(jnp.int32, (CHUNK, 1), 0)
        for g, w in enumerate(POOL_WINDOWS):
            cols = slice(g * LANES, (g + 1) * LANES)
            band = band_s[g]
            tot = (jnp.dot(band, hi_s[win, cols], preferred_element_type=F32)
                   + jnp.dot(band, lo_s[win, cols], preferred_element_type=F32))
            lo = jnp.clip(t - w // 2, 0, seq)
            hi = jnp.clip(t - w // 2 + w, 0, seq)
            pooled = tot / (hi - lo).astype(F32) - p_ref[rows, cols]
            out = jnp.dot(pooled.astype(BF16), w_ref[g].astype(BF16), preferred_element_type=F32) + b_ref[g]
            gate = p_ref[rows, D_BRANCH + g * LANES:D_BRANCH + (g + 1) * LANES]
            o_ref[rows, cols] = (out * sc_ref[:, cols] * _silu(gate)).astype(BF16)

    def blk_body(c2, carry):
        for u in range(POOL_UNROLL):
            one_chunk(c2 * POOL_UNROLL + u)
        return carry

    lax.fori_loop(0, ns * nb // POOL_UNROLL, blk_body, 0)


def _pool(proj, lp, *, n_seq, seq):
    ns = max(1, DEC_SEQ // seq)
    return pl.pallas_call(
        functools.partial(_pool_kernel, seq=seq, ns=ns),
        grid=(n_seq // ns,),
        in_specs=[
            pl.BlockSpec((ns * seq, 2 * D_BRANCH), lambda b: (b, 3)),
            pl.BlockSpec((4, LANES, LANES), lambda b: (0, 0, 0)),
            pl.BlockSpec((4, 1, LANES), lambda b: (0, 0, 0)),
            pl.BlockSpec((1, D_BRANCH), lambda b: (0, 0)),
        ],
        out_specs=pl.BlockSpec((ns * seq, D_BRANCH), lambda b: (b, 0)),
        out_shape=jax.ShapeDtypeStruct((n_seq * seq, D_BRANCH), BF16),
        scratch_shapes=[
            pltpu.VMEM((ns * (seq + 2 * POOL_HALO), D_BRANCH), BF16),
            pltpu.VMEM((ns * (seq + 2 * POOL_HALO), D_BRANCH), BF16),
            pltpu.VMEM((len(POOL_WINDOWS), CHUNK, POOL_WIN), BF16),
        ],
        compiler_params=pltpu.CompilerParams(vmem_limit_bytes=VMEM_LIMIT),
        name="pool",
    )(proj, lp["pool_w"], lp["pool_b"], lp["pool_scale"])


def _rope(x, cos, sin):
    lane = lax.broadcasted_iota(jnp.int32, x.shape, 1)
    first = (lane & 63) < 32
    swapped = jnp.where(first, pltpu.roll(x, 96, axis=1), pltpu.roll(x, 32, axis=1))
    return x * cos + swapped * sin


def _load_cache(c_ref, c_s, n_heads):
    for h in range(n_heads):
        c_s[:, h * HEAD_DIM:(h + 1) * HEAD_DIM] = c_ref[0, 0, pl.ds(h, PAST_LEN, stride=n_heads), :].astype(BF16)


def _win_attn_kernel(sink_ref, p_ref, kc_ref, vc_ref, cos_ref, sin_ref, o_ref, q_s, k_s, v_s, kc_s, vc_s):
    seq = DEC_SEQ
    nb = seq // CHUNK
    kv_w = A_KV_HEADS * HEAD_DIM
    k_off = A_HEADS * HEAD_DIM
    v_off = k_off + kv_w
    g_off = v_off + kv_w
    zero_blk = jnp.zeros((CHUNK, kv_w), BF16)
    for s in (k_s, v_s):
        s[0:CHUNK, :] = zero_blk
        s[seq + CHUNK:seq + 2 * CHUNK, :] = zero_blk
    _load_cache(kc_ref, kc_s, A_KV_HEADS)
    _load_cache(vc_ref, vc_s, A_KV_HEADS)

    def prep_body(c, carry):
        r0 = pl.multiple_of(c * CHUNK, CHUNK)
        rows = pl.ds(r0, CHUNK)
        prow = pl.ds(pl.multiple_of(r0 + CHUNK, CHUNK), CHUNK)
        cos = cos_ref[rows, :]
        sin = sin_ref[rows, :]
        for h in range(A_HEADS):
            cols = slice(h * HEAD_DIM, (h + 1) * HEAD_DIM)
            q_s[rows, cols] = _rope(p_ref[rows, cols], cos, sin).astype(BF16)
        for kk in range(A_KV_HEADS):
            cols = slice(kk * HEAD_DIM, (kk + 1) * HEAD_DIM)
            k_s[prow, cols] = _rope(p_ref[rows, k_off + kk * HEAD_DIM:k_off + (kk + 1) * HEAD_DIM], cos, sin).astype(BF16)
        v_s[prow, :] = p_ref[rows, v_off:v_off + kv_w].astype(BF16)
        return carry

    lax.fori_loop(0, nb, prep_body, 0)

    grp = A_HEADS // A_KV_HEADS

    def blk_body(n, carry):
        r0 = pl.multiple_of(n * CHUNK, CHUNK)
        rows = pl.ds(r0, CHUNK)
        win = pl.ds(r0, 3 * CHUNK)
        ii = lax.broadcasted_iota(jnp.int32, (grp * CHUNK, 3 * CHUNK), 0)
        jj = lax.broadcasted_iota(jnp.int32, (grp * CHUNK, 3 * CHUNK), 1)
        rel = jj - CHUNK - (ii & (CHUNK - 1))
        kpos = r0 - CHUNK + jj
        valid = (rel >= -A_WINDOW) & (rel <= A_WINDOW) & (kpos >= 0) & (kpos < seq)
        first_head = lax.broadcasted_iota(jnp.int32, (grp * CHUNK, 1), 0) < CHUNK
        for kk in range(A_KV_HEADS):
            kcols = slice(kk * HEAD_DIM, (kk + 1) * HEAD_DIM)
            h0 = kk * grp
            q = jnp.concatenate([q_s[rows, (h0 + g) * HEAD_DIM:(h0 + g + 1) * HEAD_DIM] for g in range(grp)], axis=0)
            s_loc = lax.dot_general(q, k_s[win, kcols], TRANS_B, preferred_element_type=F32) * ATT_SCALE
            s_loc = jnp.where(valid, s_loc, NEG_INF)
            s_ctx = lax.dot_general(q, kc_s[:, kcols], TRANS_B, preferred_element_type=F32) * ATT_SCALE
            sink = jnp.where(first_head, sink_ref[h0], sink_ref[h0 + 1])
            o = _softmax_av([s_loc, s_ctx], [v_s[win, kcols], vc_s[:, kcols]], sink)
            for g in range(grp):
                cols = slice((h0 + g) * HEAD_DIM, (h0 + g + 1) * HEAD_DIM)
                gate = p_ref[rows, g_off + (h0 + g) * HEAD_DIM:g_off + (h0 + g + 1) * HEAD_DIM]
                o_ref[rows, cols] = (o[g * CHUNK:(g + 1) * CHUNK, :] * _silu(gate)).astype(BF16)
        return carry

    lax.fori_loop(0, nb, blk_body, 0)


def _win_attn(proj, sink, kc, vc, cos_t, sin_t, layer):
    seq = DEC_SEQ
    kv_w = A_KV_HEADS * HEAD_DIM
    cache_spec = pl.BlockSpec((1, 1, PAST_LEN * A_KV_HEADS, HEAD_DIM), lambda b: (b, layer, 0, 0))
    tab_spec = pl.BlockSpec((seq, HEAD_DIM), lambda b: (0, 0))
    return pl.pallas_call(
        _win_attn_kernel,
        grid=(DEC_BATCH,),
        in_specs=[
            pl.BlockSpec(memory_space=pltpu.SMEM),
            pl.BlockSpec((seq, 3 * D_BRANCH), lambda b: (b, 0)),
            cache_spec, cache_spec, tab_spec, tab_spec,
        ],
        out_specs=pl.BlockSpec((seq, D_BRANCH), lambda b: (b, 0)),
        out_shape=jax.ShapeDtypeStruct((DEC_BATCH * seq, D_BRANCH), BF16),
        scratch_shapes=[
            pltpu.VMEM((seq, A_HEADS * HEAD_DIM), BF16),
            pltpu.VMEM((seq + 2 * CHUNK, kv_w), BF16),
            pltpu.VMEM((seq + 2 * CHUNK, kv_w), BF16),
            pltpu.VMEM((PAST_LEN, kv_w), BF16),
            pltpu.VMEM((PAST_LEN, kv_w), BF16),
        ],
        compiler_params=pltpu.CompilerParams(vmem_limit_bytes=VMEM_LIMIT),
        name="win_attn",
    )(sink, proj, kc, vc, cos_t, sin_t)


NA_ROWS = DEC_SEQ // GRID_W
NA_KROWS = min(NA_KH, NA_ROWS)
NA_PAIRS = 2 * NA_KH - 2
NA_QROWS = 4
NA_WROWS = NA_QROWS + NA_KROWS
NA_GROUPS = NA_ROWS // NA_QROWS


def _na_row_start(r):
    return min(max(r - NA_KROWS // 2, 0), NA_ROWS - NA_KROWS)


def _na_win_start(qb):
    return min(max(qb * NA_QROWS - NA_KROWS // 2, 0), NA_ROWS - NA_WROWS)


def _na_bias_kernel(rpb_ref, o_ref, pair_s):
    layer = pl.program_id(0)
    h = pl.program_id(1)
    n_dy = 2 * NA_KH - 1
    n_dx = 2 * NA_KW - 1
    base = (layer * D_HEADS + h) * (n_dy * n_dx)
    qc = lax.broadcasted_iota(jnp.int32, (GRID_W, 2 * GRID_W), 0)
    lane = lax.broadcasted_iota(jnp.int32, (GRID_W, 2 * GRID_W), 1)
    second = lane >= GRID_W
    kc = lane & (GRID_W - 1)
    idx = jnp.clip(kc - qc, -(NA_KW - 1), NA_KW - 1) + (NA_KW - 1)
    cs = jnp.clip(qc - NA_KW // 2, 0, GRID_W - NA_KW)
    col_ok = (kc >= cs) & (kc < cs + NA_KW)
    for e in range(NA_PAIRS):
        val = jnp.zeros((GRID_W, 2 * GRID_W), F32)
        for d in range(n_dx):
            r0 = rpb_ref[base + e * n_dx + d]
            r1 = rpb_ref[base + (e + 1) * n_dx + d]
            val = jnp.where(idx == d, jnp.where(second, r1, r0), val)
        pair_s[e] = jnp.where(col_ok, val, NEG_INF)

    masked = jnp.full((GRID_W, 2 * GRID_W), NEG_INF, F32)
    for qb in range(NA_GROUPS):
        ws = _na_win_start(qb)
        for ri in range(NA_QROWS):
            r = qb * NA_QROWS + ri
            rs = _na_row_start(r)
            assert ws <= rs and rs + NA_KROWS <= ws + NA_WROWS
            for p in range(NA_WROWS // 2):
                kr0 = ws + 2 * p
                ok0 = rs <= kr0 < rs + NA_KROWS
                ok1 = rs <= kr0 + 1 < rs + NA_KROWS
                e = kr0 - r + (NA_KH - 1)
                if ok0 or ok1:
                    assert 0 <= e < NA_PAIRS
                    blk = pair_s[e]
                    if not ok1:
                        blk = jnp.where(second, NEG_INF, blk)
                    if not ok0:
                        blk = jnp.where(second, blk, NEG_INF)
                else:
                    blk = masked
                o_ref[0, 0, qb, ri * GRID_W:(ri + 1) * GRID_W, p * 2 * GRID_W:(p + 1) * 2 * GRID_W] = blk


def _na_bias(na_rpb):
    blk = (NA_GROUPS, NA_QROWS * GRID_W, NA_WROWS * GRID_W)
    return pl.pallas_call(
        _na_bias_kernel,
        grid=(DEPTH, D_HEADS),
        in_specs=[pl.BlockSpec(memory_space=pltpu.SMEM)],
        out_specs=pl.BlockSpec((1, 1) + blk, lambda l, h: (l, h, 0, 0, 0)),
        out_shape=jax.ShapeDtypeStruct((DEPTH, D_HEADS) + blk, F32),
        scratch_shapes=[pltpu.VMEM((NA_PAIRS, GRID_W, 2 * GRID_W), F32)],
        name="na_bias",
    )(na_rpb.reshape(-1))


def _na_kernel(q_ref, k_ref, v_ref, g_ref, kc_ref, vc_ref, tab_ref, o_ref, k_s, v_s, kc_s, vc_s):
    h = pl.program_id(1)

    @pl.when(h == 0)
    def _():
        for hh in range(D_HEADS):
            kc_s[hh] = kc_ref[0, 0, pl.ds(hh, PAST_LEN, stride=D_HEADS), :].astype(BF16)
            vc_s[hh] = vc_ref[0, 0, pl.ds(hh, PAST_LEN, stride=D_HEADS), :].astype(BF16)

    k_s[...] = k_ref[...].astype(BF16)
    v_s[...] = v_ref[...].astype(BF16)
    kc = kc_s[h]
    vc = vc_s[h]
    nq = NA_QROWS * GRID_W
    for qb in range(NA_GROUPS):
        rows = slice(qb * nq, (qb + 1) * nq)
        ws = _na_win_start(qb)
        krows = slice(ws * GRID_W, (ws + NA_WROWS) * GRID_W)
        q = q_ref[rows, :].astype(BF16)
        s_loc = lax.dot_general(q, k_s[krows, :], TRANS_B, preferred_element_type=F32) * ATT_SCALE + tab_ref[0, 0, qb]
        s_ctx = lax.dot_general(q, kc, TRANS_B, preferred_element_type=F32) * ATT_SCALE
        o = _softmax_av([s_loc, s_ctx], [v_s[krows, :], vc])
        o_ref[rows, :] = (o * _silu(g_ref[rows, :])).astype(BF16)


def _na_attn(proj, kc, vc, tab, layer):
    seq = DEC_SEQ
    first = (D_MAIN - 4 * D_HEADS * HEAD_DIM) // HEAD_DIM
    col_spec = lambda g: pl.BlockSpec((seq, HEAD_DIM), lambda b, h: (b, first + g * D_HEADS + h))
    cache_spec = pl.BlockSpec((1, 1, PAST_LEN * D_HEADS, HEAD_DIM), lambda b, h: (b, layer, 0, 0))
    tab_blk = (NA_GROUPS, NA_QROWS * GRID_W, NA_WROWS * GRID_W)
    return pl.pallas_call(
        _na_kernel,
        grid=(DEC_BATCH, D_HEADS),
        in_specs=[
            col_spec(0), col_spec(1), col_spec(2), col_spec(3),
            cache_spec, cache_spec,
            pl.BlockSpec((1, 1) + tab_blk, lambda b, h: (layer, h, 0, 0, 0)),
        ],
        out_specs=pl.BlockSpec((seq, HEAD_DIM), lambda b, h: (b, h)),
        out_shape=jax.ShapeDtypeStruct((DEC_BATCH * seq, D_HEADS * HEAD_DIM), BF16),
        scratch_shapes=[pltpu.VMEM((seq, HEAD_DIM), BF16)] * 2 + [pltpu.VMEM((D_HEADS, PAST_LEN, HEAD_DIM), BF16)] * 2,
        compiler_params=pltpu.CompilerParams(
            dimension_semantics=("arbitrary", "arbitrary"), vmem_limit_bytes=VMEM_LIMIT),
        name="na_attn",
    )(proj, proj, proj, proj, kc, vc, tab)


def _rope_tables(seq):
    t = jnp.arange(seq)
    rows = (t // GRID_W).astype(F32)
    cols = (t % GRID_W).astype(F32)
    nf = HEAD_DIM // 4
    inv = ROPE_THETA ** (-jnp.arange(nf, dtype=F32) / nf)
    ar = rows[:, None] * inv[None, :]
    ac = cols[:, None] * inv[None, :]
    cos_t = jnp.concatenate([jnp.cos(ar), jnp.cos(ar), jnp.cos(ac), jnp.cos(ac)], axis=-1)
    sin_t = jnp.concatenate([-jnp.sin(ar), jnp.sin(ar), -jnp.sin(ac), jnp.sin(ac)], axis=-1)
    return cos_t, sin_t


def _pad_lanes(v, width=LANES):
    v = v.reshape(1, -1)
    return jnp.pad(v, ((0, 0), (0, width - v.shape[1])))


def kernel(x_prompt, x_sample, cache_attn_k, cache_attn_v, cache_na_k, cache_na_v, state_ssm_fwd, state_ssm_bwd, c, c_ctx, w_ada, b_ada, w_in, w_out, ln_g, ln_b, attn_sink, ssm_conv_w, ssm_conv_b, ssm_a_log, ssm_dt_bias, ssm_d, ssm_norm_w, pool_w, pool_b, pool_scale, na_rpb):
    alpha = (2.0 * DEPTH) ** 0.25
    n_x = B_HEADS * B_HEAD_DIM
    xc = x_prompt.reshape(BATCH * SEQ, D_MODEL)
    xl = x_sample.reshape(DEC_BATCH * DEC_SEQ, D_MODEL)

    cv = jnp.zeros((16, D_MODEL), F32).at[0].set(c_ctx).at[1:1 + DEC_BATCH].set(c)
    mod = _ada(cv, w_ada, b_ada)
    cos_t, sin_t = _rope_tables(DEC_SEQ)
    na_tab = _na_bias(na_rpb)

    kc_a = cache_attn_k.reshape(DEC_BATCH, DEPTH, PAST_LEN * A_KV_HEADS, HEAD_DIM)
    vc_a = cache_attn_v.reshape(DEC_BATCH, DEPTH, PAST_LEN * A_KV_HEADS, HEAD_DIM)
    kc_d = cache_na_k.reshape(DEC_BATCH, DEPTH, PAST_LEN * D_HEADS, HEAD_DIM)
    vc_d = cache_na_v.reshape(DEC_BATCH, DEPTH, PAST_LEN * D_HEADS, HEAD_DIM)
    h0 = (state_ssm_fwd.reshape(DEC_BATCH, DEPTH, n_x, B_STATE), state_ssm_bwd.reshape(DEC_BATCH, DEPTH, n_x, B_STATE))

    ctx_row = lambda i: 0
    lat_in_row = lambda i: 1 + i * IN_TM // DEC_SEQ
    lat_out_row = lambda i: 1 + i * OUT_TM // DEC_SEQ

    w_main, w_dt = _prep_w(w_in)
    w_o = w_out.astype(BF16)
    kv_a = kv_d = ssm_state = None
    for l in range(DEPTH):
        mod3 = mod[l].reshape(16, 1, 3 * D_MODEL)
        lp = {
            "conv_w": jnp.pad(ssm_conv_w[l], ((0, ROW_PAD - B_CONV), (0, 0))),
            "conv_b": ssm_conv_b[l].reshape(1, -1),
            "dt_bias": _pad_lanes(ssm_dt_bias[l]),
            "a_log": _pad_lanes(ssm_a_log[l]),
            "d_skip": jnp.repeat(ssm_d[l], B_HEAD_DIM).reshape(1, n_x),
            "norm_w": ssm_norm_w[l].reshape(1, n_x),
            "pool_w": pool_w[l],
            "pool_b": pool_b[l].reshape(len(POOL_WINDOWS), 1, LANES),
            "pool_scale": pool_scale[l].reshape(1, D_BRANCH),
        }
        sink = attn_sink[l]

        proj_c, dt_c = _inproj(xc, mod3, w_main, w_dt, ctx_row, l)
        o_a, *kv_a = _ctx_attn(proj_c, sink, kv_a, l, n_q=A_HEADS, n_kv=A_KV_HEADS, col_block=0,
                               width=3 * D_BRANCH, use_sink=True)
        o_b, *ssm_state = _ssm(proj_c, dt_c, lp, l, n_seq=BATCH, seq=SEQ, prev_state=ssm_state)
        o_c = _pool(proj_c, lp, n_seq=BATCH, seq=SEQ)
        o_d, *kv_d = _ctx_attn(proj_c, sink, kv_d, l, n_q=D_HEADS, n_kv=D_HEADS, col_block=2,
                               width=4 * D_BRANCH, use_sink=False)
        xc = _outproj((o_a, o_b, o_c, o_d), xc, mod3, w_o, ln_g[l], ln_b[l], ctx_row, alpha, l)

        proj_l, dt_l = _inproj(xl, mod3, w_main, w_dt, lat_in_row, l)
        o_a = _win_attn(proj_l, sink, kc_a, vc_a, cos_t, sin_t, l)
        o_b = _ssm(proj_l, dt_l, lp, l, n_seq=DEC_BATCH, seq=DEC_SEQ, h0=h0)[0]
        o_c = _pool(proj_l, lp, n_seq=DEC_BATCH, seq=DEC_SEQ)
        o_d = _na_attn(proj_l, kc_d, vc_d, na_tab, l)
        xl = _outproj((o_a, o_b, o_c, o_d), xl, mod3, w_o, ln_g[l], ln_b[l], lat_out_row, alpha, l)

    return (
        xc.reshape(BATCH, SEQ, D_MODEL),
        xl.reshape(DEC_BATCH, DEC_SEQ, D_MODEL),
        kv_a[0].reshape(BATCH, DEPTH, SEQ, A_KV_HEADS, HEAD_DIM),
        kv_a[1].reshape(BATCH, DEPTH, SEQ, A_KV_HEADS, HEAD_DIM),
        kv_d[0].reshape(BATCH, DEPTH, SEQ, D_HEADS, HEAD_DIM),
        kv_d[1].reshape(BATCH, DEPTH, SEQ, D_HEADS, HEAD_DIM),
        ssm_state[0].reshape(BATCH, DEPTH, B_HEADS, B_HEAD_DIM, B_STATE),
        ssm_state[1].reshape(BATCH, DEPTH, B_HEADS, B_HEAD_DIM, B_STATE),
    )
```

```python
import functools

import jax
import jax.numpy as jnp
from jax import lax
from jax.experimental import pallas as pl
from jax.experimental.pallas import tpu as pltpu

F32 = jnp.float32
BF16 = jnp.bfloat16

D_MODEL = 2048
BATCH = 32
SEQ = 256
DEPTH = 2
DEC_BATCH = 4
DEC_SEQ = 1024
PAST_LEN = 256
GRID_W = 64
D_BRANCH = 512
HEAD_DIM = 128
A_HEADS = 4
A_KV_HEADS = 2
A_WINDOW = 128
ROPE_THETA = 10000.0
B_HEADS = 8
B_HEAD_DIM = 64
B_STATE = 128
B_CONV = 5
CHUNK = 128
POOL_WINDOWS = (2, 4, 8, 16)
D_HEADS = 4
NA_KH = 8
NA_KW = 16
LN_EPS = 1e-6
NEG_INF = -1e30
ATT_SCALE = HEAD_DIM ** -0.5

D_MAIN = 6144
DT_OFF = 3072
DT_COLS = 16
LANES = 128
ROW_PAD = 8
VMEM_LIMIT = 56 * 1024 * 1024

TRANS_B = (((1,), (1,)), ((), ()))


def _silu(x):
    hx = 0.5 * x
    return hx + hx * jnp.tanh(hx)


def _split_bf16(a, parts):
    out = []
    rem = a
    for _ in range(parts):
        hi = rem.astype(BF16)
        out.append(hi)
        rem = rem - hi.astype(F32)
    return out


def _dot01_lhs(m01, a, parts=3):
    acc = None
    for p in _split_bf16(a, parts):
        t = jnp.dot(m01, p, preferred_element_type=F32)
        acc = t if acc is None else acc + t
    return acc


def _dot01_rhs(a, m01, parts=2):
    acc = None
    for p in _split_bf16(a, parts):
        t = jnp.dot(p, m01, preferred_element_type=F32)
        acc = t if acc is None else acc + t
    return acc


def _softmax_av(scores, values, sink=None):
    m = None
    for s in scores:
        bm = jnp.max(s, axis=-1, keepdims=True)
        m = bm if m is None else jnp.maximum(m, bm)
    if sink is not None:
        m = jnp.maximum(m, sink)
    den = None
    acc = None
    for s, v in zip(scores, values):
        p = jnp.exp(s - m)
        d = jnp.sum(p, axis=-1, keepdims=True)
        o = jnp.dot(p.astype(BF16), v, preferred_element_type=F32)
        den = d if den is None else den + d
        acc = o if acc is None else acc + o
    if sink is not None:
        den = den + jnp.exp(sink - m)
    return acc / den


def _ada_kernel(cv_ref, w_ref, b_ref, o_ref):
    a = _silu(cv_ref[...]).astype(BF16)
    w = w_ref[0].astype(BF16)
    o_ref[0] = jnp.dot(a, w, preferred_element_type=F32) + b_ref[0]


def _ada(cv, w_ada, b_ada):
    tn = 1024
    n = 3 * D_MODEL
    return pl.pallas_call(
        _ada_kernel,
        grid=(DEPTH, n // tn),
        in_specs=[
            pl.BlockSpec((16, D_MODEL), lambda l, j: (0, 0)),
            pl.BlockSpec((1, D_MODEL, tn), lambda l, j: (l, 0, j)),
            pl.BlockSpec((1, 1, tn), lambda l, j: (l, 0, j)),
        ],
        out_specs=pl.BlockSpec((1, 16, tn), lambda l, j: (l, 0, j)),
        out_shape=jax.ShapeDtypeStruct((DEPTH, 16, n), F32),
        compiler_params=pltpu.CompilerParams(vmem_limit_bytes=VMEM_LIMIT),
        name="ada_mod",
    )(cv, w_ada, b_ada.reshape(DEPTH, 1, n))


PREP_TN = 512
PREP_KC = 256
N_LOW = DT_OFF // PREP_TN


def _prep_w_kernel(a_ref, b_ref, c_ref, o_ref, odt_ref):
    j = pl.program_id(1)
    kcs = [slice(k * PREP_KC, (k + 1) * PREP_KC) for k in range(D_MODEL // PREP_KC)]

    @pl.when(j == 0)
    def _():
        row = lax.broadcasted_iota(jnp.int32, (LANES, PREP_KC), 0)
        for ks in kcs:
            odt_ref[0, ks, :] = jnp.where(row < DT_COLS, c_ref[0, :, ks], 0.0).T.astype(BF16)

    @pl.when(j < N_LOW)
    def _():
        for ks in kcs:
            o_ref[0, ks, :] = a_ref[0, :, ks].T.astype(BF16)

    @pl.when(j >= N_LOW)
    def _():
        for ks in kcs:
            src = jnp.concatenate([a_ref[0, DT_COLS:, ks], b_ref[0, :, ks]], axis=0)
            o_ref[0, ks, :] = src.T.astype(BF16)


def _prep_w(w_in):
    w_t = jnp.swapaxes(w_in, 1, 2)
    tail_blocks = PREP_TN // DT_COLS
    return pl.pallas_call(
        _prep_w_kernel,
        grid=(DEPTH, D_MAIN // PREP_TN),
        in_specs=[
            pl.BlockSpec((1, PREP_TN, D_MODEL), lambda l, j: (l, j, 0)),
            pl.BlockSpec((1, DT_COLS, D_MODEL), lambda l, j: (l, jnp.where(j >= N_LOW, (j + 1) * tail_blocks, 0), 0)),
            pl.BlockSpec((1, LANES, D_MODEL), lambda l, j: (l, DT_OFF // LANES, 0)),
        ],
        out_specs=[
            pl.BlockSpec((1, D_MODEL, PREP_TN), lambda l, j: (l, 0, j)),
            pl.BlockSpec((1, D_MODEL, LANES), lambda l, j: (l, 0, 0)),
        ],
        out_shape=[
            jax.ShapeDtypeStruct((DEPTH, D_MODEL, D_MAIN), BF16),
            jax.ShapeDtypeStruct((DEPTH, D_MODEL, LANES), BF16),
        ],
        compiler_params=pltpu.CompilerParams(vmem_limit_bytes=VMEM_LIMIT),
        name="prep_w",
    )(w_t, w_t, w_t)


IN_TM = 1024
IN_TN = 768
IN_NJ = D_MAIN // IN_TN
IN_RC = IN_TM // IN_NJ


LN_SLAB = 16


def _ln_modulate(x_ref, x_row0, u_ref, u_row0, n_rows, mod_ref):
    shift = mod_ref[0, :, 0:D_MODEL]
    scale1 = 1.0 + mod_ref[0, :, D_MODEL:2 * D_MODEL]
    for s in range(n_rows // LN_SLAB):
        xf = x_ref[pl.ds(x_row0 + s * LN_SLAB, LN_SLAB), :]
        mu = jnp.mean(xf, axis=-1, keepdims=True)
        xc = xf - mu
        var = jnp.mean(xc * xc, axis=-1, keepdims=True)
        u = xc * lax.rsqrt(var + LN_EPS) * scale1 + shift
        u_ref[pl.ds(u_row0 + s * LN_SLAB, LN_SLAB), :] = u.astype(BF16)


def _inproj_kernel(x0_ref, xn_ref, mod0_ref, modn_ref, w_ref, wdt_ref, o_ref, dt_ref, u_a, u_b):
    i = pl.program_id(0)
    j = pl.program_id(1)

    @pl.when((i == 0) & (j == 0))
    def _():
        def body(r, carry):
            r0 = pl.multiple_of(r * IN_RC, IN_RC)
            _ln_modulate(x0_ref, r0, u_a, r0, IN_RC, mod0_ref)
            return carry

        lax.fori_loop(0, IN_NJ, body, 0)

    row_j = pl.multiple_of(j * IN_RC, IN_RC)

    def step(cur, nxt):
        o_ref[...] = jnp.dot(cur[...], w_ref[0], preferred_element_type=F32)
        dt_ref[...] = jnp.dot(cur[pl.ds(row_j, IN_RC), :], wdt_ref[0], preferred_element_type=F32)
        _ln_modulate(xn_ref, 0, nxt, row_j, IN_RC, modn_ref)

    parity = lax.rem(i, 2)

    @pl.when(parity == 0)
    def _():
        step(u_a, u_b)

    @pl.when(parity == 1)
    def _():
        step(u_b, u_a)


def _inproj(x2d, mod3, w_main, w_dt, mod_row, layer):
    m = x2d.shape[0]
    n_i = m // IN_TM
    nxt = lambda i: jnp.minimum(i + 1, n_i - 1)
    return pl.pallas_call(
        _inproj_kernel,
        grid=(n_i, IN_NJ),
        in_specs=[
            pl.BlockSpec((IN_TM, D_MODEL), lambda i, j: (0, 0)),
            pl.BlockSpec((IN_RC, D_MODEL), lambda i, j: (nxt(i) * IN_NJ + j, 0)),
            pl.BlockSpec((1, 1, 3 * D_MODEL), lambda i, j: (mod_row(0), 0, 0)),
            pl.BlockSpec((1, 1, 3 * D_MODEL), lambda i, j: (mod_row(nxt(i)), 0, 0)),
            pl.BlockSpec((1, D_MODEL, IN_TN), lambda i, j: (layer, 0, j)),
            pl.BlockSpec((1, D_MODEL, LANES), lambda i, j: (layer, 0, 0)),
        ],
        out_specs=[
            pl.BlockSpec((IN_TM, IN_TN), lambda i, j: (i, j)),
            pl.BlockSpec((IN_RC, LANES), lambda i, j: (i * IN_NJ + j, 0)),
        ],
        out_shape=[
            jax.ShapeDtypeStruct((m, D_MAIN), F32),
            jax.ShapeDtypeStruct((m, LANES), F32),
        ],
        scratch_shapes=[pltpu.VMEM((IN_TM, D_MODEL), BF16)] * 2,
        compiler_params=pltpu.CompilerParams(
            dimension_semantics=("arbitrary", "arbitrary"), vmem_limit_bytes=VMEM_LIMIT),
        name="in_proj",
    )(x2d, x2d, mod3, mod3, w_main, w_dt)


OUT_TM = 512
OUT_RC = 128


def _outproj_kernel(ma_ref, mb_ref, mc_ref, md_ref, x_ref, mod_ref, w_ref, g_ref, b_ref, o_ref, acc_ref, *, alpha):
    gate = mod_ref[0, :, 2 * D_MODEL:3 * D_MODEL]
    mixed = jnp.concatenate([ma_ref[...], mb_ref[...], mc_ref[...], md_ref[...]], axis=1)
    acc_ref[...] = jnp.dot(mixed, w_ref[0], preferred_element_type=F32)

    def body(r, carry):
        rows = pl.ds(pl.multiple_of(r * OUT_RC, OUT_RC), OUT_RC)
        z = alpha * x_ref[rows, :] + acc_ref[rows, :] * gate
        mu = jnp.mean(z, axis=-1, keepdims=True)
        zc = z - mu
        var = jnp.mean(zc * zc, axis=-1, keepdims=True)
        o_ref[rows, :] = zc * lax.rsqrt(var + LN_EPS) * g_ref[...] + b_ref[...]
        return carry

    lax.fori_loop(0, OUT_TM // OUT_RC, body, 0)


def _outproj(mixed, x2d, mod3, w_o, ln_g, ln_b, mod_row, alpha, layer):
    m = x2d.shape[0]
    mspec = pl.BlockSpec((OUT_TM, D_BRANCH), lambda i: (i, 0))
    return pl.pallas_call(
        functools.partial(_outproj_kernel, alpha=alpha),
        grid=(m // OUT_TM,),
        in_specs=[
            mspec, mspec, mspec, mspec,
            pl.BlockSpec((OUT_TM, D_MODEL), lambda i: (i, 0)),
            pl.BlockSpec((1, 1, 3 * D_MODEL), lambda i: (mod_row(i), 0, 0)),
            pl.BlockSpec((1, D_MODEL, D_MODEL), lambda i: (layer, 0, 0)),
            pl.BlockSpec((1, D_MODEL), lambda i: (0, 0)),
            pl.BlockSpec((1, D_MODEL), lambda i: (0, 0)),
        ],
        out_specs=pl.BlockSpec((OUT_TM, D_MODEL), lambda i: (i, 0)),
        out_shape=jax.ShapeDtypeStruct((m, D_MODEL), F32),
        scratch_shapes=[pltpu.VMEM((OUT_TM, D_MODEL), F32)],
        compiler_params=pltpu.CompilerParams(vmem_limit_bytes=VMEM_LIMIT),
        name="out_proj",
    )(*mixed, x2d, mod3, w_o, ln_g.reshape(1, D_MODEL), ln_b.reshape(1, D_MODEL))


CTX_NS = 2


def _ctx_attn_kernel(*refs, n_q, n_kv, use_sink, aliased):
    sink_ref, p_ref = refs[0], refs[1]
    o_ref, kn_ref, vn_ref = refs[4:7] if aliased else refs[2:5]
    grp = n_q // n_kv
    k_off = n_q * HEAD_DIM
    v_off = k_off + n_kv * HEAD_DIM
    g_off = v_off + n_kv * HEAD_DIM
    for s_i in range(CTX_NS):
        rows = slice(s_i * SEQ, (s_i + 1) * SEQ)
        for kk in range(n_kv):
            k32 = p_ref[rows, k_off + kk * HEAD_DIM:k_off + (kk + 1) * HEAD_DIM]
            v32 = p_ref[rows, v_off + kk * HEAD_DIM:v_off + (kk + 1) * HEAD_DIM]
            kn_ref[s_i, 0, pl.ds(kk, SEQ, stride=n_kv), :] = k32
            vn_ref[s_i, 0, pl.ds(kk, SEQ, stride=n_kv), :] = v32
            k = k32.astype(BF16)
            v = v32.astype(BF16)
            for g in range(grp):
                h = kk * grp + g
                cols = slice(h * HEAD_DIM, (h + 1) * HEAD_DIM)
                q = p_ref[rows, cols].astype(BF16)
                s = lax.dot_general(q, k, TRANS_B, preferred_element_type=F32) * ATT_SCALE
                o = _softmax_av([s], [v], sink_ref[h] if use_sink else None)
                gate = p_ref[rows, g_off + h * HEAD_DIM:g_off + (h + 1) * HEAD_DIM]
                o_ref[rows, cols] = (o * _silu(gate)).astype(BF16)


def _ctx_attn(proj, sink, prev_kv, layer, *, n_q, n_kv, col_block, width, use_sink):
    aliased = prev_kv is not None
    kv_spec = pl.BlockSpec((CTX_NS, 1, SEQ * n_kv, HEAD_DIM), lambda b: (b, layer, 0, 0))
    kv_shape = jax.ShapeDtypeStruct((BATCH, DEPTH, SEQ * n_kv, HEAD_DIM), F32)
    in_specs = [
        pl.BlockSpec(memory_space=pltpu.SMEM),
        pl.BlockSpec((CTX_NS * SEQ, width), lambda b: (b, col_block)),
    ]
    args = [sink, proj]
    if aliased:
        in_specs += [pl.BlockSpec(memory_space=pl.ANY)] * 2
        args += list(prev_kv)
    return pl.pallas_call(
        functools.partial(_ctx_attn_kernel, n_q=n_q, n_kv=n_kv, use_sink=use_sink, aliased=aliased),
        grid=(BATCH // CTX_NS,),
        in_specs=in_specs,
        out_specs=[pl.BlockSpec((CTX_NS * SEQ, n_q * HEAD_DIM), lambda b: (b, 0)), kv_spec, kv_spec],
        out_shape=[jax.ShapeDtypeStruct((BATCH * SEQ, n_q * HEAD_DIM), BF16), kv_shape, kv_shape],
        input_output_aliases={2: 1, 3: 2} if aliased else {},
        compiler_params=pltpu.CompilerParams(vmem_limit_bytes=VMEM_LIMIT),
        name="ctx_attn",
    )(*args)


SSM_ROWS_PER_STEP = 512
CONV_WIN = CHUNK + 2 * ROW_PAD


class _SsmSeq:
    def __init__(self, s_i, seq, p_ref, dt_ref, o_ref, state_refs, scratch):
        rows = pl.ds(s_i * seq, seq)
        self.p = p_ref.at[rows, :]
        self.dt = dt_ref.at[rows, :]
        self.o = o_ref.at[rows, :]
        self.h0f, self.h0b, self.hf, self.hb = (None if r is None else r.at[s_i, 0] for r in state_refs)
        self.xpad, self.xbc_s, self.y_s, self.eb_s, self.sb_s, self.db_s, self.ht_s = (r.at[s_i] for r in scratch)


def _ssm_kernel(*refs, seq, nsq, has_h0, emit_state, aliased):
    nc = seq // CHUNK
    it = iter(refs)
    p_ref, dt_ref, cw_ref, cb_ref, dtb_ref, alog_ref, dskip_ref, nw_ref = (next(it) for _ in range(8))
    h0f_ref = next(it) if has_h0 else None
    h0b_ref = next(it) if has_h0 else None
    if aliased:
        next(it), next(it)
    o_ref = next(it)
    hf_ref = next(it) if emit_state else None
    hb_ref = next(it) if emit_state else None
    scratch = [next(it) for _ in range(7)]
    ex_s = next(it)
    seqs = [_SsmSeq(s_i, seq, p_ref, dt_ref, o_ref, (h0f_ref, h0b_ref, hf_ref, hb_ref), scratch)
            for s_i in range(nsq)]

    def for_each_seq(fn):
        def body(c, carry):
            for sq in seqs:
                fn(sq, c)
            return carry
        return body

    n_x = B_HEADS * B_HEAD_DIM
    n_xbc = 2 * n_x

    @pl.when(pl.program_id(0) == 0)
    def _():
        r_i = lax.broadcasted_iota(jnp.int32, (LANES, 2 * n_x), 0)
        c_i = lax.broadcasted_iota(jnp.int32, (LANES, 2 * n_x), 1)
        ex_s[...] = jnp.where(r_i == lax.shift_right_logical(c_i, 6), 1.0, 0.0).astype(BF16)

    zero_rows = jnp.zeros((ROW_PAD, n_xbc), F32)
    for sq in seqs:
        sq.xpad[0:ROW_PAD, :] = zero_rows
        sq.xpad[seq + ROW_PAD:seq + 2 * ROW_PAD, :] = zero_rows

    def copy_chunk(sq, c):
        r0 = pl.multiple_of(c * CHUNK, CHUNK)
        sq.xpad[pl.ds(pl.multiple_of(r0 + ROW_PAD, ROW_PAD), CHUNK), :] = sq.p[pl.ds(r0, CHUNK), 0:n_xbc]

    lax.fori_loop(0, nc, for_each_seq(copy_chunk), 0)

    def load_state(sq, h_view):
        for blk in range(4):
            cols = slice(blk * LANES, (blk + 1) * LANES)
            if h_view is None:
                sq.ht_s[:, cols] = jnp.zeros((B_STATE, LANES), F32)
            else:
                sq.ht_s[:, cols] = h_view[cols, :].T

    def store_state(sq, h_view):
        for blk in range(4):
            cols = slice(blk * LANES, (blk + 1) * LANES)
            h_view[cols, :] = sq.ht_s[:, cols].T

    for sq in seqs:
        load_state(sq, sq.h0f)

    def fwd_chunk(sq, c):
        r0 = pl.multiple_of(c * CHUNK, CHUNK)
        rows = pl.ds(r0, CHUNK)
        win = sq.xpad[pl.ds(r0, CONV_WIN), :]
        acc = jnp.broadcast_to(cb_ref[...], (CHUNK, n_xbc))
        for k in range(B_CONV):
            sh = (B_CONV // 2 - k) % CONV_WIN
            rolled = win if sh == 0 else pltpu.roll(win, sh, axis=0)
            acc = acc + cw_ref[k:k + 1, :] * rolled[ROW_PAD:ROW_PAD + CHUNK, :]
        xbc = _silu(acc)
        sq.xbc_s[rows, :] = xbc
        xs = xbc[:, 0:n_x]
        bm = xbc[:, n_x:n_x + 2 * B_STATE]
        cm = xbc[:, n_x + 2 * B_STATE:n_xbc]

        dtr = sq.dt[rows, :] + dtb_ref[...]
        dt = jnp.maximum(dtr, 0.0) + jnp.log1p(jnp.exp(-jnp.abs(dtr)))
        a = dt * (-jnp.exp(alog_ref[...]))
        ii = lax.broadcasted_iota(jnp.int32, (CHUNK, CHUNK), 0)
        jj = lax.broadcasted_iota(jnp.int32, (CHUNK, CHUNK), 1)
        tril = jj <= ii
        triu = jj >= ii
        lc_f = _dot01_lhs(jnp.where(tril, 1.0, 0.0).astype(BF16), a)
        lc_b = _dot01_lhs(jnp.where(triu, 1.0, 0.0).astype(BF16), a)
        fwd_lane = jj < B_HEADS
        lc = jnp.where(fwd_lane, lc_f, lc_b)
        lend = jnp.where(fwd_lane[0:1, :], lc[CHUNK - 1:CHUNK, :], lc[0:1, :])
        toend = jnp.exp(lend - lc) * dt
        e_exp = _dot01_rhs(jnp.exp(lc), ex_s[...])
        w_exp = _dot01_rhs(toend, ex_s[...])
        d_exp = _dot01_rhs(jnp.broadcast_to(jnp.exp(lend), (ROW_PAD, LANES)), ex_s[...])
        lc_t = lc.T
        dt_t = dt.T
        sq.eb_s[rows, :] = e_exp[:, n_x:2 * n_x]
        sq.db_s[c] = d_exp[:, n_x:2 * n_x]

        lane_g = lax.broadcasted_iota(jnp.int32, (CHUNK, 2 * LANES), 1)
        for g in range(2):
            gcols = slice(g * 2 * LANES, (g + 1) * 2 * LANES)
            cg = cm[:, g * B_STATE:(g + 1) * B_STATE].astype(BF16)
            bg = bm[:, g * B_STATE:(g + 1) * B_STATE]
            cb = lax.dot_general(cg, bg.astype(BF16), TRANS_B, preferred_element_type=F32)
            bg_t = bg.T.astype(BF16)
            xg = xs[:, gcols]
            xblk = jnp.concatenate(
                [jnp.where((lane_g >= hh * B_HEAD_DIM) & (lane_g < (hh + 1) * B_HEAD_DIM), xg, 0.0).astype(BF16)
                 for hh in range(4)], axis=0)
            for d in range(2):
                mask = tril if d == 0 else triu
                ms = []
                for hh in range(4):
                    col = d * B_HEADS + g * 4 + hh
                    seg = lc[:, col:col + 1] - lc_t[col:col + 1, :]
                    dec = jnp.exp(jnp.where(mask, seg, -jnp.inf))
                    ms.append((cb * dec * dt_t[col:col + 1, :]).astype(BF16))
                y_in = jnp.dot(jnp.concatenate(ms, axis=1), xblk, preferred_element_type=F32)
                wg = w_exp[:, d * n_x + g * 2 * LANES:d * n_x + (g + 1) * 2 * LANES]
                st = jnp.dot(bg_t, (xg * wg).astype(BF16), preferred_element_type=F32)
                if d == 0:
                    h_prev = sq.ht_s[:, gcols]
                    y_x = e_exp[:, gcols] * jnp.dot(cg, h_prev.astype(BF16), preferred_element_type=F32)
                    sq.y_s[rows, gcols] = y_in + y_x
                    sq.ht_s[:, gcols] = d_exp[0:1, gcols] * h_prev + st
                else:
                    sq.y_s[rows, gcols] = sq.y_s[rows, gcols] + y_in
                    sq.sb_s[c, :, gcols] = st

    lax.fori_loop(0, nc, for_each_seq(fwd_chunk), 0)
    for sq in seqs:
        if emit_state:
            store_state(sq, sq.hf)
        load_state(sq, sq.h0b)

    def bwd_chunk(sq, i):
        c = nc - 1 - i
        r0 = pl.multiple_of(c * CHUNK, CHUNK)
        rows = pl.ds(r0, CHUNK)
        for g in range(2):
            gcols = slice(g * 2 * LANES, (g + 1) * 2 * LANES)
            cg = sq.xbc_s[rows, n_x + 2 * B_STATE + g * B_STATE:n_x + 2 * B_STATE + (g + 1) * B_STATE].astype(BF16)
            h_prev = sq.ht_s[:, gcols]
            y_x = sq.eb_s[rows, gcols] * jnp.dot(cg, h_prev.astype(BF16), preferred_element_type=F32)
            sq.y_s[rows, gcols] = sq.y_s[rows, gcols] + y_x
            sq.ht_s[:, gcols] = sq.db_s[c, 0:1, gcols] * h_prev + sq.sb_s[c, :, gcols]

    lax.fori_loop(0, nc, for_each_seq(bwd_chunk), 0)
    if emit_state:
        for sq in seqs:
            store_state(sq, sq.hb)

    def out_chunk(sq, c):
        rows = pl.ds(pl.multiple_of(c * CHUNK, CHUNK), CHUNK)
        y = sq.y_s[rows, :] + dskip_ref[...] * sq.xbc_s[rows, 0:n_x]
        yz = y * _silu(sq.p[rows, n_xbc:n_xbc + n_x])
        ms = jnp.mean(yz * yz, axis=-1, keepdims=True)
        sq.o[rows, :] = (yz * lax.rsqrt(ms + LN_EPS) * nw_ref[...]).astype(BF16)

    lax.fori_loop(0, nc, for_each_seq(out_chunk), 0)


def _ssm(proj, dt, lp, layer, *, n_seq, seq, h0=None, prev_state=None):
    has_h0 = h0 is not None
    emit_state = not has_h0
    aliased = prev_state is not None
    nc = seq // CHUNK
    nsq = max(1, SSM_ROWS_PER_STEP // seq)
    n_x = B_HEADS * B_HEAD_DIM
    const2 = lambda b: (0, 0)
    in_specs = [
        pl.BlockSpec((nsq * seq, 3 * n_x), lambda b: (b, 1)),
        pl.BlockSpec((nsq * seq, LANES), lambda b: (b, 0)),
        pl.BlockSpec((ROW_PAD, 2 * n_x), const2),
        pl.BlockSpec((1, 2 * n_x), const2),
        pl.BlockSpec((1, LANES), const2),
        pl.BlockSpec((1, LANES), const2),
        pl.BlockSpec((1, n_x), const2),
        pl.BlockSpec((1, n_x), const2),
    ]
    args = [proj, dt, lp["conv_w"], lp["conv_b"], lp["dt_bias"], lp["a_log"], lp["d_skip"], lp["norm_w"]]
    state_spec = pl.BlockSpec((nsq, 1, n_x, B_STATE), lambda b: (b, layer, 0, 0))
    if has_h0:
        in_specs += [state_spec, state_spec]
        args += list(h0)
    aliases = {}
    if aliased:
        aliases = {len(args): 1, len(args) + 1: 2}
        in_specs += [pl.BlockSpec(memory_space=pl.ANY)] * 2
        args += list(prev_state)
    out_specs = [pl.BlockSpec((nsq * seq, n_x), lambda b: (b, 0))]
    out_shape = [jax.ShapeDtypeStruct((n_seq * seq, n_x), BF16)]
    if emit_state:
        out_specs += [state_spec, state_spec]
        out_shape += [jax.ShapeDtypeStruct((n_seq, DEPTH, n_x, B_STATE), F32)] * 2
    return pl.pallas_call(
        functools.partial(_ssm_kernel, seq=seq, nsq=nsq, has_h0=has_h0, emit_state=emit_state, aliased=aliased),
        grid=(n_seq // nsq,),
        in_specs=in_specs,
        out_specs=out_specs,
        out_shape=out_shape,
        input_output_aliases=aliases,
        scratch_shapes=[
            pltpu.VMEM((nsq, seq + 2 * ROW_PAD, 2 * n_x), F32),
            pltpu.VMEM((nsq, seq, 2 * n_x), F32),
            pltpu.VMEM((nsq, seq, n_x), F32),
            pltpu.VMEM((nsq, seq, n_x), F32),
            pltpu.VMEM((nsq, nc, B_STATE, n_x), F32),
            pltpu.VMEM((nsq, nc, ROW_PAD, n_x), F32),
            pltpu.VMEM((nsq, B_STATE, n_x), F32),
            pltpu.VMEM((LANES, 2 * n_x), BF16),
        ],
        compiler_params=pltpu.CompilerParams(vmem_limit_bytes=VMEM_LIMIT),
        name="ssm",
    )(*args)


POOL_HALO = 16
POOL_WIN = CHUNK + 2 * POOL_HALO
POOL_UNROLL = 2


def _pool_kernel(p_ref, w_ref, b_ref, sc_ref, o_ref, hi_s, lo_s, band_s, *, seq, ns):
    nb = seq // CHUNK
    slot = seq + 2 * POOL_HALO

    def locate(c):
        s_i, cb = (0, c) if ns == 1 else (c // nb, c % nb)
        r0 = pl.multiple_of(c * CHUNK, CHUNK)
        return r0, pl.multiple_of(s_i * slot + cb * CHUNK, POOL_HALO), cb * CHUNK

    @pl.when(pl.program_id(0) == 0)
    def _():
        ii = lax.broadcasted_iota(jnp.int32, (CHUNK, POOL_WIN), 0)
        jj = lax.broadcasted_iota(jnp.int32, (CHUNK, POOL_WIN), 1)
        rel = jj - POOL_HALO - ii
        for g, w in enumerate(POOL_WINDOWS):
            band_s[g] = jnp.where((rel >= -(w // 2)) & (rel < w - w // 2), 1.0, 0.0).astype(BF16)

    zero_blk = jnp.zeros((POOL_HALO, D_BRANCH), BF16)
    for s in (hi_s, lo_s):
        for s_i in range(ns):
            s[s_i * slot:s_i * slot + POOL_HALO, :] = zero_blk
            s[(s_i + 1) * slot - POOL_HALO:(s_i + 1) * slot, :] = zero_blk

    def split_body(c, carry):
        r0, w0, _ = locate(c)
        dst = pl.ds(pl.multiple_of(w0 + POOL_HALO, POOL_HALO), CHUNK)
        hi, lo = _split_bf16(p_ref[pl.ds(r0, CHUNK), 0:D_BRANCH], 2)
        hi_s[dst, :] = hi
        lo_s[dst, :] = lo
        return carry

    lax.fori_loop(0, ns * nb, split_body, 0)

    def one_chunk(c):
        r0, w0, t0 = locate(c)
        rows = pl.ds(r0, CHUNK)
        win = pl.ds(w0, POOL_WIN)
        t = t0 + lax.broadcasted_iota(jnp.int32, (CHUNK, 1), 0)
        for g, w in enumerate(POOL_WINDOWS):
            cols = slice(g * LANES, (g + 1) * LANES)
            band = band_s[g]
            tot = (jnp.dot(band, hi_s[win, cols], preferred_element_type=F32)
                   + jnp.dot(band, lo_s[win, cols], preferred_element_type=F32))
            lo = jnp.clip(t - w // 2, 0, seq)
            hi = jnp.clip(t - w // 2 + w, 0, seq)
            pooled = tot / (hi - lo).astype(F32) - p_ref[rows, cols]
            out = jnp.dot(pooled.astype(BF16), w_ref[g].astype(BF16), preferred_element_type=F32) + b_ref[g]
            gate = p_ref[rows, D_BRANCH + g * LANES:D_BRANCH + (g + 1) * LANES]
            o_ref[rows, cols] = (out * sc_ref[:, cols] * _silu(gate)).astype(BF16)

    def blk_body(c2, carry):
        for u in range(POOL_UNROLL):
            one_chunk(c2 * POOL_UNROLL + u)
        return carry

    lax.fori_loop(0, ns * nb // POOL_UNROLL, blk_body, 0)


def _pool(proj, lp, *, n_seq, seq):
    ns = max(1, DEC_SEQ // seq)
    return pl.pallas_call(
        functools.partial(_pool_kernel, seq=seq, ns=ns),
        grid=(n_seq // ns,),
        in_specs=[
            pl.BlockSpec((ns * seq, 2 * D_BRANCH), lambda b: (b, 3)),
            pl.BlockSpec((4, LANES, LANES), lambda b: (0, 0, 0)),
            pl.BlockSpec((4, 1, LANES), lambda b: (0, 0, 0)),
            pl.BlockSpec((1, D_BRANCH), lambda b: (0, 0)),
        ],
        out_specs=pl.BlockSpec((ns * seq, D_BRANCH), lambda b: (b, 0)),
        out_shape=jax.ShapeDtypeStruct((n_seq * seq, D_BRANCH), BF16),
        scratch_shapes=[
            pltpu.VMEM((ns * (seq + 2 * POOL_HALO), D_BRANCH), BF16),
            pltpu.VMEM((ns * (seq + 2 * POOL_HALO), D_BRANCH), BF16),
            pltpu.VMEM((len(POOL_WINDOWS), CHUNK, POOL_WIN), BF16),
        ],
        compiler_params=pltpu.CompilerParams(vmem_limit_bytes=VMEM_LIMIT),
        name="pool",
    )(proj, lp["pool_w"], lp["pool_b"], lp["pool_scale"])


def _rope(x, cos, sin):
    lane = lax.broadcasted_iota(jnp.int32, x.shape, 1)
    first = (lane & 63) < 32
    swapped = jnp.where(first, pltpu.roll(x, 96, axis=1), pltpu.roll(x, 32, axis=1))
    return x * cos + swapped * sin


def _load_cache(c_ref, c_s, n_heads):
    for h in range(n_heads):
        c_s[:, h * HEAD_DIM:(h + 1) * HEAD_DIM] = c_ref[0, 0, pl.ds(h, PAST_LEN, stride=n_heads), :].astype(BF16)


def _win_attn_kernel(sink_ref, p_ref, kc_ref, vc_ref, cos_ref, sin_ref, o_ref, q_s, k_s, v_s, kc_s, vc_s):
    seq = DEC_SEQ
    nb = seq // CHUNK
    kv_w = A_KV_HEADS * HEAD_DIM
    k_off = A_HEADS * HEAD_DIM
    v_off = k_off + kv_w
    g_off = v_off + kv_w
    zero_blk = jnp.zeros((CHUNK, kv_w), BF16)
    for s in (k_s, v_s):
        s[0:CHUNK, :] = zero_blk
        s[seq + CHUNK:seq + 2 * CHUNK, :] = zero_blk
    _load_cache(kc_ref, kc_s, A_KV_HEADS)
    _load_cache(vc_ref, vc_s, A_KV_HEADS)

    def prep_body(c, carry):
        r0 = pl.multiple_of(c * CHUNK, CHUNK)
        rows = pl.ds(r0, CHUNK)
        prow = pl.ds(pl.multiple_of(r0 + CHUNK, CHUNK), CHUNK)
        cos = cos_ref[rows, :]
        sin = sin_ref[rows, :]
        for h in range(A_HEADS):
            cols = slice(h * HEAD_DIM, (h + 1) * HEAD_DIM)
            q_s[rows, cols] = _rope(p_ref[rows, cols], cos, sin).astype(BF16)
        for kk in range(A_KV_HEADS):
            cols = slice(kk * HEAD_DIM, (kk + 1) * HEAD_DIM)
            k_s[prow, cols] = _rope(p_ref[rows, k_off + kk * HEAD_DIM:k_off + (kk + 1) * HEAD_DIM], cos, sin).astype(BF16)
        v_s[prow, :] = p_ref[rows, v_off:v_off + kv_w].astype(BF16)
        return carry

    lax.fori_loop(0, nb, prep_body, 0)

    grp = A_HEADS // A_KV_HEADS

    def blk_body(n, carry):
        r0 = pl.multiple_of(n * CHUNK, CHUNK)
        rows = pl.ds(r0, CHUNK)
        win = pl.ds(r0, 3 * CHUNK)
        ii = lax.broadcasted_iota(jnp.int32, (grp * CHUNK, 3 * CHUNK), 0)
        jj = lax.broadcasted_iota(jnp.int32, (grp * CHUNK, 3 * CHUNK), 1)
        rel = jj - CHUNK - (ii & (CHUNK - 1))
        kpos = r0 - CHUNK + jj
        valid = (rel >= -A_WINDOW) & (rel <= A_WINDOW) & (kpos >= 0) & (kpos < seq)
        first_head = lax.broadcasted_iota(jnp.int32, (grp * CHUNK, 1), 0) < CHUNK
        for kk in range(A_KV_HEADS):
            kcols = slice(kk * HEAD_DIM, (kk + 1) * HEAD_DIM)
            h0 = kk * grp
            q = jnp.concatenate([q_s[rows, (h0 + g) * HEAD_DIM:(h0 + g + 1) * HEAD_DIM] for g in range(grp)], axis=0)
            s_loc = lax.dot_general(q, k_s[win, kcols], TRANS_B, preferred_element_type=F32) * ATT_SCALE
            s_loc = jnp.where(valid, s_loc, NEG_INF)
            s_ctx = lax.dot_general(q, kc_s[:, kcols], TRANS_B, preferred_element_type=F32) * ATT_SCALE
            sink = jnp.where(first_head, sink_ref[h0], sink_ref[h0 + 1])
            o = _softmax_av([s_loc, s_ctx], [v_s[win, kcols], vc_s[:, kcols]], sink)
            for g in range(grp):
                cols = slice((h0 + g) * HEAD_DIM, (h0 + g + 1) * HEAD_DIM)
                gate = p_ref[rows, g_off + (h0 + g) * HEAD_DIM:g_off + (h0 + g + 1) * HEAD_DIM]
                o_ref[rows, cols] = (o[g * CHUNK:(g + 1) * CHUNK, :] * _silu(gate)).astype(BF16)
        return carry

    lax.fori_loop(0, nb, blk_body, 0)


def _win_attn(proj, sink, kc, vc, cos_t, sin_t, layer):
    seq = DEC_SEQ
    kv_w = A_KV_HEADS * HEAD_DIM
    cache_spec = pl.BlockSpec((1, 1, PAST_LEN * A_KV_HEADS, HEAD_DIM), lambda b: (b, layer, 0, 0))
    tab_spec = pl.BlockSpec((seq, HEAD_DIM), lambda b: (0, 0))
    return pl.pallas_call(
        _win_attn_kernel,
        grid=(DEC_BATCH,),
        in_specs=[
            pl.BlockSpec(memory_space=pltpu.SMEM),
            pl.BlockSpec((seq, 3 * D_BRANCH), lambda b: (b, 0)),
            cache_spec, cache_spec, tab_spec, tab_spec,
        ],
        out_specs=pl.BlockSpec((seq, D_BRANCH), lambda b: (b, 0)),
        out_shape=jax.ShapeDtypeStruct((DEC_BATCH * seq, D_BRANCH), BF16),
        scratch_shapes=[
            pltpu.VMEM((seq, A_HEADS * HEAD_DIM), BF16),
            pltpu.VMEM((seq + 2 * CHUNK, kv_w), BF16),
            pltpu.VMEM((seq + 2 * CHUNK, kv_w), BF16),
            pltpu.VMEM((PAST_LEN, kv_w), BF16),
            pltpu.VMEM((PAST_LEN, kv_w), BF16),
        ],
        compiler_params=pltpu.CompilerParams(vmem_limit_bytes=VMEM_LIMIT),
        name="win_attn",
    )(sink, proj, kc, vc, cos_t, sin_t)


NA_ROWS = DEC_SEQ // GRID_W
NA_KROWS = min(NA_KH, NA_ROWS)
NA_PAIRS = 2 * NA_KH - 2
NA_QROWS = 4
NA_WROWS = NA_QROWS + NA_KROWS
NA_GROUPS = NA_ROWS // NA_QROWS


def _na_row_start(r):
    return min(max(r - NA_KROWS // 2, 0), NA_ROWS - NA_KROWS)


def _na_win_start(qb):
    return min(max(qb * NA_QROWS - NA_KROWS // 2, 0), NA_ROWS - NA_WROWS)


def _na_bias_kernel(rpb_ref, o_ref, pair_s):
    layer = pl.program_id(0)
    h = pl.program_id(1)
    n_dy = 2 * NA_KH - 1
    n_dx = 2 * NA_KW - 1
    base = (layer * D_HEADS + h) * (n_dy * n_dx)
    qc = lax.broadcasted_iota(jnp.int32, (GRID_W, 2 * GRID_W), 0)
    lane = lax.broadcasted_iota(jnp.int32, (GRID_W, 2 * GRID_W), 1)
    second = lane >= GRID_W
    kc = lane & (GRID_W - 1)
    idx = jnp.clip(kc - qc, -(NA_KW - 1), NA_KW - 1) + (NA_KW - 1)
    cs = jnp.clip(qc - NA_KW // 2, 0, GRID_W - NA_KW)
    col_ok = (kc >= cs) & (kc < cs + NA_KW)
    for e in range(NA_PAIRS):
        val = jnp.zeros((GRID_W, 2 * GRID_W), F32)
        for d in range(n_dx):
            r0 = rpb_ref[base + e * n_dx + d]
            r1 = rpb_ref[base + (e + 1) * n_dx + d]
            val = jnp.where(idx == d, jnp.where(second, r1, r0), val)
        pair_s[e] = jnp.where(col_ok, val, NEG_INF)

    masked = jnp.full((GRID_W, 2 * GRID_W), NEG_INF, F32)
    for qb in range(NA_GROUPS):
        ws = _na_win_start(qb)
        for ri in range(NA_QROWS):
            r = qb * NA_QROWS + ri
            rs = _na_row_start(r)
            assert ws <= rs and rs + NA_KROWS <= ws + NA_WROWS
            for p in range(NA_WROWS // 2):
                kr0 = ws + 2 * p
                ok0 = rs <= kr0 < rs + NA_KROWS
                ok1 = rs <= kr0 + 1 < rs + NA_KROWS
                e = kr0 - r + (NA_KH - 1)
                if ok0 or ok1:
                    assert 0 <= e < NA_PAIRS
                    blk = pair_s[e]
                    if not ok1:
                        blk = jnp.where(second, NEG_INF, blk)
                    if not ok0:
                        blk = jnp.where(second, blk, NEG_INF)
                else:
                    blk = masked
                o_ref[0, 0, qb, ri * GRID_W:(ri + 1) * GRID_W, p * 2 * GRID_W:(p + 1) * 2 * GRID_W] = blk


def _na_bias(na_rpb):
    blk = (NA_GROUPS, NA_QROWS * GRID_W, NA_WROWS * GRID_W)
    return pl.pallas_call(
        _na_bias_kernel,
        grid=(DEPTH, D_HEADS),
        in_specs=[pl.BlockSpec(memory_space=pltpu.SMEM)],
        out_specs=pl.BlockSpec((1, 1) + blk, lambda l, h: (l, h, 0, 0, 0)),
        out_shape=jax.ShapeDtypeStruct((DEPTH, D_HEADS) + blk, F32),
        scratch_shapes=[pltpu.VMEM((NA_PAIRS, GRID_W, 2 * GRID_W), F32)],
        name="na_bias",
    )(na_rpb.reshape(-1))


def _na_kernel(q_ref, k_ref, v_ref, g_ref, kc_ref, vc_ref, tab_ref, o_ref, k_s, v_s, kc_s, vc_s):
    h = pl.program_id(1)

    @pl.when(h == 0)
    def _():
        for hh in range(D_HEADS):
            kc_s[hh] = kc_ref[0, 0, pl.ds(hh, PAST_LEN, stride=D_HEADS), :].astype(BF16)
            vc_s[hh] = vc_ref[0, 0, pl.ds(hh, PAST_LEN, stride=D_HEADS), :].astype(BF16)

    k_s[...] = k_ref[...].astype(BF16)
    v_s[...] = v_ref[...].astype(BF16)
    kc = kc_s[h]
    vc = vc_s[h]
    nq = NA_QROWS * GRID_W
    for qb in range(NA_GROUPS):
        rows = slice(qb * nq, (qb + 1) * nq)
        ws = _na_win_start(qb)
        krows = slice(ws * GRID_W, (ws + NA_WROWS) * GRID_W)
        q = q_ref[rows, :].astype(BF16)
        s_loc = lax.dot_general(q, k_s[krows, :], TRANS_B, preferred_element_type=F32) * ATT_SCALE + tab_ref[0, 0, qb]
        s_ctx = lax.dot_general(q, kc, TRANS_B, preferred_element_type=F32) * ATT_SCALE
        o = _softmax_av([s_loc, s_ctx], [v_s[krows, :], vc])
        o_ref[rows, :] = (o * _silu(g_ref[rows, :])).astype(BF16)


def _na_attn(proj, kc, vc, tab, layer):
    seq = DEC_SEQ
    first = (D_MAIN - 4 * D_HEADS * HEAD_DIM) // HEAD_DIM
    col_spec = lambda g: pl.BlockSpec((seq, HEAD_DIM), lambda b, h: (b, first + g * D_HEADS + h))
    cache_spec = pl.BlockSpec((1, 1, PAST_LEN * D_HEADS, HEAD_DIM), lambda b, h: (b, layer, 0, 0))
    tab_blk = (NA_GROUPS, NA_QROWS * GRID_W, NA_WROWS * GRID_W)
    return pl.pallas_call(
        _na_kernel,
        grid=(DEC_BATCH, D_HEADS),
        in_specs=[
            col_spec(0), col_spec(1), col_spec(2), col_spec(3),
            cache_spec, cache_spec,
            pl.BlockSpec((1, 1) + tab_blk, lambda b, h: (layer, h, 0, 0, 0)),
        ],
        out_specs=pl.BlockSpec((seq, HEAD_DIM), lambda b, h: (b, h)),
        out_shape=jax.ShapeDtypeStruct((DEC_BATCH * seq, D_HEADS * HEAD_DIM), BF16),
        scratch_shapes=[pltpu.VMEM((seq, HEAD_DIM), BF16)] * 2 + [pltpu.VMEM((D_HEADS, PAST_LEN, HEAD_DIM), BF16)] * 2,
        compiler_params=pltpu.CompilerParams(
            dimension_semantics=("arbitrary", "arbitrary"), vmem_limit_bytes=VMEM_LIMIT),
        name="na_attn",
    )(proj, proj, proj, proj, kc, vc, tab)


def _rope_tables(seq):
    t = jnp.arange(seq)
    rows = (t // GRID_W).astype(F32)
    cols = (t % GRID_W).astype(F32)
    nf = HEAD_DIM // 4
    inv = ROPE_THETA ** (-jnp.arange(nf, dtype=F32) / nf)
    ar = rows[:, None] * inv[None, :]
    ac = cols[:, None] * inv[None, :]
    cos_t = jnp.concatenate([jnp.cos(ar), jnp.cos(ar), jnp.cos(ac), jnp.cos(ac)], axis=-1)
    sin_t = jnp.concatenate([-jnp.sin(ar), jnp.sin(ar), -jnp.sin(ac), jnp.sin(ac)], axis=-1)
    return cos_t, sin_t


def _pad_lanes(v, width=LANES):
    v = v.reshape(1, -1)
    return jnp.pad(v, ((0, 0), (0, width - v.shape[1])))


def kernel(x_prompt, x_sample, cache_attn_k, cache_attn_v, cache_na_k, cache_na_v, state_ssm_fwd, state_ssm_bwd, c, c_ctx, w_ada, b_ada, w_in, w_out, ln_g, ln_b, attn_sink, ssm_conv_w, ssm_conv_b, ssm_a_log, ssm_dt_bias, ssm_d, ssm_norm_w, pool_w, pool_b, pool_scale, na_rpb):
    alpha = (2.0 * DEPTH) ** 0.25
    n_x = B_HEADS * B_HEAD_DIM
    xc = x_prompt.reshape(BATCH * SEQ, D_MODEL)
    xl = x_sample.reshape(DEC_BATCH * DEC_SEQ, D_MODEL)

    cv = jnp.zeros((16, D_MODEL), F32).at[0].set(c_ctx).at[1:1 + DEC_BATCH].set(c)
    mod = _ada(cv, w_ada, b_ada)
    cos_t, sin_t = _rope_tables(DEC_SEQ)
    na_tab = _na_bias(na_rpb)

    kc_a = cache_attn_k.reshape(DEC_BATCH, DEPTH, PAST_LEN * A_KV_HEADS, HEAD_DIM)
    vc_a = cache_attn_v.reshape(DEC_BATCH, DEPTH, PAST_LEN * A_KV_HEADS, HEAD_DIM)
    kc_d = cache_na_k.reshape(DEC_BATCH, DEPTH, PAST_LEN * D_HEADS, HEAD_DIM)
    vc_d = cache_na_v.reshape(DEC_BATCH, DEPTH, PAST_LEN * D_HEADS, HEAD_DIM)
    h0 = (state_ssm_fwd.reshape(DEC_BATCH, DEPTH, n_x, B_STATE), state_ssm_bwd.reshape(DEC_BATCH, DEPTH, n_x, B_STATE))

    ctx_row = lambda i: 0
    lat_in_row = lambda i: 1 + i * IN_TM // DEC_SEQ
    lat_out_row = lambda i: 1 + i * OUT_TM // DEC_SEQ

    w_main, w_dt = _prep_w(w_in)
    w_o = w_out.astype(BF16)
    kv_a = kv_d = ssm_state = None
    for l in range(DEPTH):
        mod3 = mod[l].reshape(16, 1, 3 * D_MODEL)
        lp = {
            "conv_w": jnp.pad(ssm_conv_w[l], ((0, ROW_PAD - B_CONV), (0, 0))),
            "conv_b": ssm_conv_b[l].reshape(1, -1),
            "dt_bias": _pad_lanes(ssm_dt_bias[l]),
            "a_log": _pad_lanes(ssm_a_log[l]),
            "d_skip": jnp.repeat(ssm_d[l], B_HEAD_DIM).reshape(1, n_x),
            "norm_w": ssm_norm_w[l].reshape(1, n_x),
            "pool_w": pool_w[l],
            "pool_b": pool_b[l].reshape(len(POOL_WINDOWS), 1, LANES),
            "pool_scale": pool_scale[l].reshape(1, D_BRANCH),
        }
        sink = attn_sink[l]

        proj_c, dt_c = _inproj(xc, mod3, w_main, w_dt, ctx_row, l)
        o_a, *kv_a = _ctx_attn(proj_c, sink, kv_a, l, n_q=A_HEADS, n_kv=A_KV_HEADS, col_block=0,
                               width=3 * D_BRANCH, use_sink=True)
        o_b, *ssm_state = _ssm(proj_c, dt_c, lp, l, n_seq=BATCH, seq=SEQ, prev_state=ssm_state)
        o_c = _pool(proj_c, lp, n_seq=BATCH, seq=SEQ)
        o_d, *kv_d = _ctx_attn(proj_c, sink, kv_d, l, n_q=D_HEADS, n_kv=D_HEADS, col_block=2,
                               width=4 * D_BRANCH, use_sink=False)
        xc = _outproj((o_a, o_b, o_c, o_d), xc, mod3, w_o, ln_g[l], ln_b[l], ctx_row, alpha, l)

        proj_l, dt_l = _inproj(xl, mod3, w_main, w_dt, lat_in_row, l)
        o_a = _win_attn(proj_l, sink, kc_a, vc_a, cos_t, sin_t, l)
        o_b = _ssm(proj_l, dt_l, lp, l, n_seq=DEC_BATCH, seq=DEC_SEQ, h0=h0)[0]
        o_c = _pool(proj_l, lp, n_seq=DEC_BATCH, seq=DEC_SEQ)
        o_d = _na_attn(proj_l, kc_d, vc_d, na_tab, l)
        xl = _outproj((o_a, o_b, o_c, o_d), xl, mod3, w_o, ln_g[l], ln_b[l], lat_out_row, alpha, l)

    return (
        xc.reshape(BATCH, SEQ, D_MODEL),
        xl.reshape(DEC_BATCH, DEC_SEQ, D_MODEL),
        kv_a[0].reshape(BATCH, DEPTH, SEQ, A_KV_HEADS, HEAD_DIM),
        kv_a[1].reshape(BATCH, DEPTH, SEQ, A_KV_HEADS, HEAD_DIM),
        kv_d[0].reshape(BATCH, DEPTH, SEQ, D_HEADS, HEAD_DIM),
        kv_d[1].reshape(BATCH, DEPTH, SEQ, D_HEADS, HEAD_DIM),
        ssm_state[0].reshape(BATCH, DEPTH, B_HEADS, B_HEAD_DIM, B_STATE),
        ssm_state[1].reshape(BATCH, DEPTH, B_HEADS, B_HEAD_DIM, B_STATE),
    )
```

```python
import functools

import jax
import jax.numpy as jnp
from jax import lax
from jax.experimental import pallas as pl
from jax.experimental.pallas import tpu as pltpu

F32 = jnp.float32
BF16 = jnp.bfloat16

D_MODEL = 2048
BATCH = 32
SEQ = 256
DEPTH = 2
DEC_BATCH = 4
DEC_SEQ = 1024
PAST_LEN = 256
GRID_W = 64
D_BRANCH = 512
HEAD_DIM = 128
A_HEADS = 4
A_KV_HEADS = 2
A_WINDOW = 128
ROPE_THETA = 10000.0
B_HEADS = 8
B_HEAD_DIM = 64
B_STATE = 128
B_CONV = 5
CHUNK = 128
POOL_WINDOWS = (2, 4, 8, 16)
D_HEADS = 4
NA_KH = 8
NA_KW = 16
LN_EPS = 1e-6
NEG_INF = -1e30
ATT_SCALE = HEAD_DIM ** -0.5
EXP2_SCALE = ATT_SCALE * 1.4426950408889634

D_MAIN = 6144
DT_OFF = 3072
DT_COLS = 16
LANES = 128
ROW_PAD = 8
VMEM_LIMIT = 56 * 1024 * 1024

TRANS_B = (((1,), (1,)), ((), ()))


def _silu(x):
    hx = 0.5 * x
    return hx + hx * jnp.tanh(hx)


def _split_bf16(a, parts):
    out = []
    rem = a
    for _ in range(parts):
        hi = rem.astype(BF16)
        out.append(hi)
        rem = rem - hi.astype(F32)
    return out


def _dot01_lhs(m01, a, parts=3):
    acc = None
    for p in _split_bf16(a, parts):
        t = jnp.dot(m01, p, preferred_element_type=F32)
        acc = t if acc is None else acc + t
    return acc


def _dot01_rhs(a, m01, parts=2):
    acc = None
    for p in _split_bf16(a, parts):
        t = jnp.dot(p, m01, preferred_element_type=F32)
        acc = t if acc is None else acc + t
    return acc


def _softmax_av(scores, values, sink=None):
    m = None
    for s in scores:
        bm = jnp.max(s, axis=-1, keepdims=True)
        m = bm if m is None else jnp.maximum(m, bm)
    if sink is not None:
        sink = sink * (1.0 / ATT_SCALE)
        m = jnp.maximum(m, sink)
    den = None
    acc = None
    for s, v in zip(scores, values):
        p = jnp.exp2((s - m) * EXP2_SCALE)
        d = jnp.sum(p, axis=-1, keepdims=True)
        o = jnp.dot(p.astype(BF16), v, preferred_element_type=F32)
        den = d if den is None else den + d
        acc = o if acc is None else acc + o
    if sink is not None:
        den = den + jnp.exp2((sink - m) * EXP2_SCALE)
    return acc / den


def _ada_kernel(cv_ref, w_ref, b_ref, o_ref):
    a = _silu(cv_ref[...]).astype(BF16)
    w = w_ref[0].astype(BF16)
    o_ref[0] = jnp.dot(a, w, preferred_element_type=F32) + b_ref[0]


def _ada(cv, w_ada, b_ada):
    tn = 1024
    n = 3 * D_MODEL
    return pl.pallas_call(
        _ada_kernel,
        grid=(DEPTH, n // tn),
        in_specs=[
            pl.BlockSpec((16, D_MODEL), lambda l, j: (0, 0)),
            pl.BlockSpec((1, D_MODEL, tn), lambda l, j: (l, 0, j)),
            pl.BlockSpec((1, 1, tn), lambda l, j: (l, 0, j)),
        ],
        out_specs=pl.BlockSpec((1, 16, tn), lambda l, j: (l, 0, j)),
        out_shape=jax.ShapeDtypeStruct((DEPTH, 16, n), F32),
        compiler_params=pltpu.CompilerParams(vmem_limit_bytes=VMEM_LIMIT),
        name="ada_mod",
    )(cv, w_ada, b_ada.reshape(DEPTH, 1, n))


PREP_TN = 512
PREP_KC = 256
N_LOW = DT_OFF // PREP_TN


def _prep_w_kernel(a_ref, b_ref, c_ref, o_ref, odt_ref):
    j = pl.program_id(1)
    kcs = [slice(k * PREP_KC, (k + 1) * PREP_KC) for k in range(D_MODEL // PREP_KC)]

    @pl.when(j == 0)
    def _():
        row = lax.broadcasted_iota(jnp.int32, (LANES, PREP_KC), 0)
        for ks in kcs:
            odt_ref[0, ks, :] = jnp.where(row < DT_COLS, c_ref[0, :, ks], 0.0).T.astype(BF16)

    @pl.when(j < N_LOW)
    def _():
        for ks in kcs:
            o_ref[0, ks, :] = a_ref[0, :, ks].T.astype(BF16)

    @pl.when(j >= N_LOW)
    def _():
        for ks in kcs:
            src = jnp.concatenate([a_ref[0, DT_COLS:, ks], b_ref[0, :, ks]], axis=0)
            o_ref[0, ks, :] = src.T.astype(BF16)


def _prep_w(w_in):
    w_t = jnp.swapaxes(w_in, 1, 2)
    tail_blocks = PREP_TN // DT_COLS
    return pl.pallas_call(
        _prep_w_kernel,
        grid=(DEPTH, D_MAIN // PREP_TN),
        in_specs=[
            pl.BlockSpec((1, PREP_TN, D_MODEL), lambda l, j: (l, j, 0)),
            pl.BlockSpec((1, DT_COLS, D_MODEL), lambda l, j: (l, jnp.where(j >= N_LOW, (j + 1) * tail_blocks, 0), 0)),
            pl.BlockSpec((1, LANES, D_MODEL), lambda l, j: (l, DT_OFF // LANES, 0)),
        ],
        out_specs=[
            pl.BlockSpec((1, D_MODEL, PREP_TN), lambda l, j: (l, 0, j)),
            pl.BlockSpec((1, D_MODEL, LANES), lambda l, j: (l, 0, 0)),
        ],
        out_shape=[
            jax.ShapeDtypeStruct((DEPTH, D_MODEL, D_MAIN), BF16),
            jax.ShapeDtypeStruct((DEPTH, D_MODEL, LANES), BF16),
        ],
        compiler_params=pltpu.CompilerParams(vmem_limit_bytes=VMEM_LIMIT),
        name="prep_w",
    )(w_t, w_t, w_t)


IN_TM = 1024
IN_TN = 768
IN_NJ = D_MAIN // IN_TN
IN_RC = IN_TM // IN_NJ


LN_SLAB = 16


def _ln_modulate(x_ref, x_row0, u_ref, u_row0, n_rows, mod_ref):
    shift = mod_ref[0, :, 0:D_MODEL]
    scale1 = 1.0 + mod_ref[0, :, D_MODEL:2 * D_MODEL]
    for s in range(n_rows // LN_SLAB):
        xf = x_ref[pl.ds(x_row0 + s * LN_SLAB, LN_SLAB), :]
        mu = jnp.mean(xf, axis=-1, keepdims=True)
        xc = xf - mu
        var = jnp.mean(xc * xc, axis=-1, keepdims=True)
        u = xc * lax.rsqrt(var + LN_EPS) * scale1 + shift
        u_ref[pl.ds(u_row0 + s * LN_SLAB, LN_SLAB), :] = u.astype(BF16)


def _inproj_kernel(x0_ref, xn_ref, mod0_ref, modn_ref, w_ref, wdt_ref, o_ref, dt_ref, u_a, u_b):
    i = pl.program_id(0)
    j = pl.program_id(1)

    @pl.when((i == 0) & (j == 0))
    def _():
        def body(r, carry):
            r0 = pl.multiple_of(r * IN_RC, IN_RC)
            _ln_modulate(x0_ref, r0, u_a, r0, IN_RC, mod0_ref)
            return carry

        lax.fori_loop(0, IN_NJ, body, 0)

    row_j = pl.multiple_of(j * IN_RC, IN_RC)

    def step(cur, nxt):
        o_ref[...] = jnp.dot(cur[...], w_ref[0], preferred_element_type=F32)
        dt_ref[...] = jnp.dot(cur[pl.ds(row_j, IN_RC), :], wdt_ref[0], preferred_element_type=F32)
        _ln_modulate(xn_ref, 0, nxt, row_j, IN_RC, modn_ref)

    parity = lax.rem(i, 2)

    @pl.when(parity == 0)
    def _():
        step(u_a, u_b)

    @pl.when(parity == 1)
    def _():
        step(u_b, u_a)


def _inproj(x2d, mod3, w_main, w_dt, mod_row, layer):
    m = x2d.shape[0]
    n_i = m // IN_TM
    nxt = lambda i: jnp.minimum(i + 1, n_i - 1)
    return pl.pallas_call(
        _inproj_kernel,
        grid=(n_i, IN_NJ),
        in_specs=[
            pl.BlockSpec((IN_TM, D_MODEL), lambda i, j: (0, 0)),
            pl.BlockSpec((IN_RC, D_MODEL), lambda i, j: (nxt(i) * IN_NJ + j, 0)),
            pl.BlockSpec((1, 1, 3 * D_MODEL), lambda i, j: (mod_row(0), 0, 0)),
            pl.BlockSpec((1, 1, 3 * D_MODEL), lambda i, j: (mod_row(nxt(i)), 0, 0)),
            pl.BlockSpec((1, D_MODEL, IN_TN), lambda i, j: (layer, 0, j)),
            pl.BlockSpec((1, D_MODEL, LANES), lambda i, j: (layer, 0, 0)),
        ],
        out_specs=[
            pl.BlockSpec((IN_TM, IN_TN), lambda i, j: (i, j)),
            pl.BlockSpec((IN_RC, LANES), lambda i, j: (i * IN_NJ + j, 0)),
        ],
        out_shape=[
            jax.ShapeDtypeStruct((m, D_MAIN), F32),
            jax.ShapeDtypeStruct((m, LANES), F32),
        ],
        scratch_shapes=[pltpu.VMEM((IN_TM, D_MODEL), BF16)] * 2,
        compiler_params=pltpu.CompilerParams(
            dimension_semantics=("arbitrary", "arbitrary"), vmem_limit_bytes=VMEM_LIMIT),
        name="in_proj",
    )(x2d, x2d, mod3, mod3, w_main, w_dt)


OUT_TM = 512
OUT_RC = 128


def _outproj_kernel(ma_ref, mb_ref, mc_ref, md_ref, x_ref, mod_ref, w_ref, g_ref, b_ref, o_ref, acc_ref, *, alpha):
    gate = mod_ref[0, :, 2 * D_MODEL:3 * D_MODEL]
    mixed = jnp.concatenate([ma_ref[...], mb_ref[...], mc_ref[...], md_ref[...]], axis=1)
    acc_ref[...] = jnp.dot(mixed, w_ref[0], preferred_element_type=F32)

    def body(r, carry):
        rows = pl.ds(pl.multiple_of(r * OUT_RC, OUT_RC), OUT_RC)
        z = alpha * x_ref[rows, :] + acc_ref[rows, :] * gate
        mu = jnp.mean(z, axis=-1, keepdims=True)
        zc = z - mu
        var = jnp.mean(zc * zc, axis=-1, keepdims=True)
        o_ref[rows, :] = zc * lax.rsqrt(var + LN_EPS) * g_ref[...] + b_ref[...]
        return carry

    lax.fori_loop(0, OUT_TM // OUT_RC, body, 0)


def _outproj(mixed, x2d, mod3, w_o, ln_g, ln_b, mod_row, alpha, layer):
    m = x2d.shape[0]
    mspec = pl.BlockSpec((OUT_TM, D_BRANCH), lambda i: (i, 0))
    return pl.pallas_call(
        functools.partial(_outproj_kernel, alpha=alpha),
        grid=(m // OUT_TM,),
        in_specs=[
            mspec, mspec, mspec, mspec,
            pl.BlockSpec((OUT_TM, D_MODEL), lambda i: (i, 0)),
            pl.BlockSpec((1, 1, 3 * D_MODEL), lambda i: (mod_row(i), 0, 0)),
            pl.BlockSpec((1, D_MODEL, D_MODEL), lambda i: (layer, 0, 0)),
            pl.BlockSpec((1, D_MODEL), lambda i: (0, 0)),
            pl.BlockSpec((1, D_MODEL), lambda i: (0, 0)),
        ],
        out_specs=pl.BlockSpec((OUT_TM, D_MODEL), lambda i: (i, 0)),
        out_shape=jax.ShapeDtypeStruct((m, D_MODEL), F32),
        scratch_shapes=[pltpu.VMEM((OUT_TM, D_MODEL), F32)],
        compiler_params=pltpu.CompilerParams(vmem_limit_bytes=VMEM_LIMIT),
        name="out_proj",
    )(*mixed, x2d, mod3, w_o, ln_g.reshape(1, D_MODEL), ln_b.reshape(1, D_MODEL))


CTX_NS = 2


def _ctx_attn_kernel(*refs, n_q, n_kv, use_sink, aliased):
    sink_ref, p_ref = refs[0], refs[1]
    o_ref, kn_ref, vn_ref = refs[4:7] if aliased else refs[2:5]
    grp = n_q // n_kv
    k_off = n_q * HEAD_DIM
    v_off = k_off + n_kv * HEAD_DIM
    g_off = v_off + n_kv * HEAD_DIM
    for s_i in range(CTX_NS):
        rows = slice(s_i * SEQ, (s_i + 1) * SEQ)
        for kk in range(n_kv):
            k32 = p_ref[rows, k_off + kk * HEAD_DIM:k_off + (kk + 1) * HEAD_DIM]
            v32 = p_ref[rows, v_off + kk * HEAD_DIM:v_off + (kk + 1) * HEAD_DIM]
            kn_ref[s_i, 0, pl.ds(kk, SEQ, stride=n_kv), :] = k32
            vn_ref[s_i, 0, pl.ds(kk, SEQ, stride=n_kv), :] = v32
            k = k32.astype(BF16)
            v = v32.astype(BF16)
            for g in range(grp):
                h = kk * grp + g
                cols = slice(h * HEAD_DIM, (h + 1) * HEAD_DIM)
                q = p_ref[rows, cols].astype(BF16)
                s = lax.dot_general(q, k, TRANS_B, preferred_element_type=F32)
                o = _softmax_av([s], [v], sink_ref[h] if use_sink else None)
                gate = p_ref[rows, g_off + h * HEAD_DIM:g_off + (h + 1) * HEAD_DIM]
                o_ref[rows, cols] = (o * _silu(gate)).astype(BF16)


def _ctx_attn(proj, sink, prev_kv, layer, *, n_q, n_kv, col_block, width, use_sink):
    aliased = prev_kv is not None
    kv_spec = pl.BlockSpec((CTX_NS, 1, SEQ * n_kv, HEAD_DIM), lambda b: (b, layer, 0, 0))
    kv_shape = jax.ShapeDtypeStruct((BATCH, DEPTH, SEQ * n_kv, HEAD_DIM), F32)
    in_specs = [
        pl.BlockSpec(memory_space=pltpu.SMEM),
        pl.BlockSpec((CTX_NS * SEQ, width), lambda b: (b, col_block)),
    ]
    args = [sink, proj]
    if aliased:
        in_specs += [pl.BlockSpec(memory_space=pl.ANY)] * 2
        args += list(prev_kv)
    return pl.pallas_call(
        functools.partial(_ctx_attn_kernel, n_q=n_q, n_kv=n_kv, use_sink=use_sink, aliased=aliased),
        grid=(BATCH // CTX_NS,),
        in_specs=in_specs,
        out_specs=[pl.BlockSpec((CTX_NS * SEQ, n_q * HEAD_DIM), lambda b: (b, 0)), kv_spec, kv_spec],
        out_shape=[jax.ShapeDtypeStruct((BATCH * SEQ, n_q * HEAD_DIM), BF16), kv_shape, kv_shape],
        input_output_aliases={2: 1, 3: 2} if aliased else {},
        compiler_params=pltpu.CompilerParams(vmem_limit_bytes=VMEM_LIMIT),
        name="ctx_attn",
    )(*args)


SSM_ROWS_PER_STEP = 512
CONV_WIN = CHUNK + 2 * ROW_PAD


class _SsmSeq:
    def __init__(self, s_i, seq, p_ref, dt_ref, o_ref, state_refs, scratch):
        rows = pl.ds(s_i * seq, seq)
        self.p = p_ref.at[rows, :]
        self.dt = dt_ref.at[rows, :]
        self.o = o_ref.at[rows, :]
        self.h0f, self.h0b, self.hf, self.hb = (None if r is None else r.at[s_i, 0] for r in state_refs)
        self.xpad, self.xbc_s, self.y_s, self.eb_s, self.sb_s, self.db_s, self.ht_s = (r.at[s_i] for r in scratch)


def _ssm_kernel(*refs, seq, nsq, has_h0, emit_state, aliased):
    nc = seq // CHUNK
    it = iter(refs)
    p_ref, dt_ref, cw_ref, cb_ref, dtb_ref, alog_ref, dskip_ref, nw_ref = (next(it) for _ in range(8))
    h0f_ref = next(it) if has_h0 else None
    h0b_ref = next(it) if has_h0 else None
    if aliased:
        next(it), next(it)
    o_ref = next(it)
    hf_ref = next(it) if emit_state else None
    hb_ref = next(it) if emit_state else None
    scratch = [next(it) for _ in range(7)]
    ex_s = next(it)
    seqs = [_SsmSeq(s_i, seq, p_ref, dt_ref, o_ref, (h0f_ref, h0b_ref, hf_ref, hb_ref), scratch)
            for s_i in range(nsq)]

    def for_each_seq(fn):
        def body(c, carry):
            for sq in seqs:
                fn(sq, c)
            return carry
        return body

    n_x = B_HEADS * B_HEAD_DIM
    n_xbc = 2 * n_x

    @pl.when(pl.program_id(0) == 0)
    def _():
        r_i = lax.broadcasted_iota(jnp.int32, (LANES, 2 * n_x), 0)
        c_i = lax.broadcasted_iota(jnp.int32, (LANES, 2 * n_x), 1)
        ex_s[...] = jnp.where(r_i == lax.shift_right_logical(c_i, 6), 1.0, 0.0).astype(BF16)

    zero_rows = jnp.zeros((ROW_PAD, n_xbc), F32)
    for sq in seqs:
        sq.xpad[0:ROW_PAD, :] = zero_rows
        sq.xpad[seq + ROW_PAD:seq + 2 * ROW_PAD, :] = zero_rows

    def copy_chunk(sq, c):
        r0 = pl.multiple_of(c * CHUNK, CHUNK)
        sq.xpad[pl.ds(pl.multiple_of(r0 + ROW_PAD, ROW_PAD), CHUNK), :] = sq.p[pl.ds(r0, CHUNK), 0:n_xbc]

    lax.fori_loop(0, nc, for_each_seq(copy_chunk), 0)

    def load_state(sq, h_view):
        for blk in range(4):
            cols = slice(blk * LANES, (blk + 1) * LANES)
            if h_view is None:
                sq.ht_s[:, cols] = jnp.zeros((B_STATE, LANES), F32)
            else:
                sq.ht_s[:, cols] = h_view[cols, :].T

    def store_state(sq, h_view):
        for blk in range(4):
            cols = slice(blk * LANES, (blk + 1) * LANES)
            h_view[cols, :] = sq.ht_s[:, cols].T

    for sq in seqs:
        load_state(sq, sq.h0f)

    def fwd_chunk(sq, c):
        r0 = pl.multiple_of(c * CHUNK, CHUNK)
        rows = pl.ds(r0, CHUNK)
        win = sq.xpad[pl.ds(r0, CONV_WIN), :]
        acc = jnp.broadcast_to(cb_ref[...], (CHUNK, n_xbc))
        for k in range(B_CONV):
            sh = (B_CONV // 2 - k) % CONV_WIN
            rolled = win if sh == 0 else pltpu.roll(win, sh, axis=0)
            acc = acc + cw_ref[k:k + 1, :] * rolled[ROW_PAD:ROW_PAD + CHUNK, :]
        xbc = _silu(acc)
        sq.xbc_s[rows, :] = xbc
        xs = xbc[:, 0:n_x]
        bm = xbc[:, n_x:n_x + 2 * B_STATE]
        cm = xbc[:, n_x + 2 * B_STATE:n_xbc]

        dtr = sq.dt[rows, :] + dtb_ref[...]
        dt = jnp.maximum(dtr, 0.0) + jnp.log1p(jnp.exp(-jnp.abs(dtr)))
        a = dt * (-jnp.exp(alog_ref[...]))
        ii = lax.broadcasted_iota(jnp.int32, (CHUNK, CHUNK), 0)
        jj = lax.broadcasted_iota(jnp.int32, (CHUNK, CHUNK), 1)
        tril = jj <= ii
        triu = jj >= ii
        lc_f = _dot01_lhs(jnp.where(tril, 1.0, 0.0).astype(BF16), a)
        lc_b = _dot01_lhs(jnp.where(triu, 1.0, 0.0).astype(BF16), a)
        fwd_lane = jj < B_HEADS
        lc = jnp.where(fwd_lane, lc_f, lc_b)
        lend = jnp.where(fwd_lane[0:1, :], lc[CHUNK - 1:CHUNK, :], lc[0:1, :])
        toend = jnp.exp(lend - lc) * dt
        e_exp = _dot01_rhs(jnp.exp(lc), ex_s[...])
        w_exp = _dot01_rhs(toend, ex_s[...])
        d_exp = _dot01_rhs(jnp.broadcast_to(jnp.exp(lend), (ROW_PAD, LANES)), ex_s[...])
        lc_t = lc.T
        dt_t = dt.T
        sq.eb_s[rows, :] = e_exp[:, n_x:2 * n_x]
        sq.db_s[c] = d_exp[:, n_x:2 * n_x]

        lane_g = lax.broadcasted_iota(jnp.int32, (CHUNK, 2 * LANES), 1)
        for g in range(2):
            gcols = slice(g * 2 * LANES, (g + 1) * 2 * LANES)
            cg = cm[:, g * B_STATE:(g + 1) * B_STATE].astype(BF16)
            bg = bm[:, g * B_STATE:(g + 1) * B_STATE]
            cb = lax.dot_general(cg, bg.astype(BF16), TRANS_B, preferred_element_type=F32)
            bg_t = bg.T.astype(BF16)
            xg = xs[:, gcols]
            xblk = jnp.concatenate(
                [jnp.where((lane_g >= hh * B_HEAD_DIM) & (lane_g < (hh + 1) * B_HEAD_DIM), xg, 0.0).astype(BF16)
                 for hh in range(4)], axis=0)
            for d in range(2):
                mask = tril if d == 0 else triu
                ms = []
                for hh in range(4):
                    col = d * B_HEADS + g * 4 + hh
                    seg = lc[:, col:col + 1] - lc_t[col:col + 1, :]
                    dec = jnp.exp(jnp.where(mask, seg, -jnp.inf))
                    ms.append((cb * dec * dt_t[col:col + 1, :]).astype(BF16))
                y_in = jnp.dot(jnp.concatenate(ms, axis=1), xblk, preferred_element_type=F32)
                wg = w_exp[:, d * n_x + g * 2 * LANES:d * n_x + (g + 1) * 2 * LANES]
                st = jnp.dot(bg_t, (xg * wg).astype(BF16), preferred_element_type=F32)
                if d == 0:
                    h_prev = sq.ht_s[:, gcols]
                    y_x = e_exp[:, gcols] * jnp.dot(cg, h_prev.astype(BF16), preferred_element_type=F32)
                    sq.y_s[rows, gcols] = y_in + y_x
                    sq.ht_s[:, gcols] = d_exp[0:1, gcols] * h_prev + st
                else:
                    sq.y_s[rows, gcols] = sq.y_s[rows, gcols] + y_in
                    sq.sb_s[c, :, gcols] = st

    lax.fori_loop(0, nc, for_each_seq(fwd_chunk), 0)
    for sq in seqs:
        if emit_state:
            store_state(sq, sq.hf)
        load_state(sq, sq.h0b)

    def bwd_chunk(sq, i):
        c = nc - 1 - i
        r0 = pl.multiple_of(c * CHUNK, CHUNK)
        rows = pl.ds(r0, CHUNK)
        for g in range(2):
            gcols = slice(g * 2 * LANES, (g + 1) * 2 * LANES)
            cg = sq.xbc_s[rows, n_x + 2 * B_STATE + g * B_STATE:n_x + 2 * B_STATE + (g + 1) * B_STATE].astype(BF16)
            h_prev = sq.ht_s[:, gcols]
            y_x = sq.eb_s[rows, gcols] * jnp.dot(cg, h_prev.astype(BF16), preferred_element_type=F32)
            sq.y_s[rows, gcols] = sq.y_s[rows, gcols] + y_x
            sq.ht_s[:, gcols] = sq.db_s[c, 0:1, gcols] * h_prev + sq.sb_s[c, :, gcols]

    lax.fori_loop(0, nc, for_each_seq(bwd_chunk), 0)
    if emit_state:
        for sq in seqs:
            store_state(sq, sq.hb)

    def out_chunk(sq, c):
        rows = pl.ds(pl.multiple_of(c * CHUNK, CHUNK), CHUNK)
        y = sq.y_s[rows, :] + dskip_ref[...] * sq.xbc_s[rows, 0:n_x]
        yz = y * _silu(sq.p[rows, n_xbc:n_xbc + n_x])
        ms = jnp.mean(yz * yz, axis=-1, keepdims=True)
        sq.o[rows, :] = (yz * lax.rsqrt(ms + LN_EPS) * nw_ref[...]).astype(BF16)

    lax.fori_loop(0, nc, for_each_seq(out_chunk), 0)


def _ssm(proj, dt, lp, layer, *, n_seq, seq, h0=None, prev_state=None):
    has_h0 = h0 is not None
    emit_state = not has_h0
    aliased = prev_state is not None
    nc = seq // CHUNK
    nsq = max(1, SSM_ROWS_PER_STEP // seq)
    n_x = B_HEADS * B_HEAD_DIM
    const2 = lambda b: (0, 0)
    in_specs = [
        pl.BlockSpec((nsq * seq, 3 * n_x), lambda b: (b, 1)),
        pl.BlockSpec((nsq * seq, LANES), lambda b: (b, 0)),
        pl.BlockSpec((ROW_PAD, 2 * n_x), const2),
        pl.BlockSpec((1, 2 * n_x), const2),
        pl.BlockSpec((1, LANES), const2),
        pl.BlockSpec((1, LANES), const2),
        pl.BlockSpec((1, n_x), const2),
        pl.BlockSpec((1, n_x), const2),
    ]
    args = [proj, dt, lp["conv_w"], lp["conv_b"], lp["dt_bias"], lp["a_log"], lp["d_skip"], lp["norm_w"]]
    state_spec = pl.BlockSpec((nsq, 1, n_x, B_STATE), lambda b: (b, layer, 0, 0))
    if has_h0:
        in_specs += [state_spec, state_spec]
        args += list(h0)
    aliases = {}
    if aliased:
        aliases = {len(args): 1, len(args) + 1: 2}
        in_specs += [pl.BlockSpec(memory_space=pl.ANY)] * 2
        args += list(prev_state)
    out_specs = [pl.BlockSpec((nsq * seq, n_x), lambda b: (b, 0))]
    out_shape = [jax.ShapeDtypeStruct((n_seq * seq, n_x), BF16)]
    if emit_state:
        out_specs += [state_spec, state_spec]
        out_shape += [jax.ShapeDtypeStruct((n_seq, DEPTH, n_x, B_STATE), F32)] * 2
    return pl.pallas_call(
        functools.partial(_ssm_kernel, seq=seq, nsq=nsq, has_h0=has_h0, emit_state=emit_state, aliased=aliased),
        grid=(n_seq // nsq,),
        in_specs=in_specs,
        out_specs=out_specs,
        out_shape=out_shape,
        input_output_aliases=aliases,
        scratch_shapes=[
            pltpu.VMEM((nsq, seq + 2 * ROW_PAD, 2 * n_x), F32),
            pltpu.VMEM((nsq, seq, 2 * n_x), F32),
            pltpu.VMEM((nsq, seq, n_x), F32),
            pltpu.VMEM((nsq, seq, n_x), F32),
            pltpu.VMEM((nsq, nc, B_STATE, n_x), F32),
            pltpu.VMEM((nsq, nc, ROW_PAD, n_x), F32),
            pltpu.VMEM((nsq, B_STATE, n_x), F32),
            pltpu.VMEM((LANES, 2 * n_x), BF16),
        ],
        compiler_params=pltpu.CompilerParams(vmem_limit_bytes=VMEM_LIMIT),
        name="ssm",
    )(*args)


POOL_HALO = 16
POOL_WIN = CHUNK + 2 * POOL_HALO
POOL_UNROLL = 2


def _pool_kernel(p_ref, w_ref, b_ref, sc_ref, o_ref, hi_s, lo_s, band_s, wbd_s, *, seq, ns):
    nb = seq // CHUNK
    slot = seq + 2 * POOL_HALO

    def locate(c):
        s_i, cb = (0, c) if ns == 1 else (c // nb, c % nb)
        r0 = pl.multiple_of(c * CHUNK, CHUNK)
        return r0, pl.multiple_of(s_i * slot + cb * CHUNK, POOL_HALO), cb * CHUNK

    @pl.when(pl.program_id(0) == 0)
    def _():
        ii = lax.broadcasted_iota(jnp.int32, (CHUNK, POOL_WIN), 0)
        jj = lax.broadcasted_iota(jnp.int32, (CHUNK, POOL_WIN), 1)
        rel = jj - POOL_HALO - ii
        for g, w in enumerate(POOL_WINDOWS):
            band_s[g] = jnp.where((rel >= -(w // 2)) & (rel < w - w // 2), 1.0, 0.0).astype(BF16)
        wbd_s[...] = jnp.zeros((D_BRANCH, D_BRANCH), BF16)
        for g in range(len(POOL_WINDOWS)):
            cols = slice(g * LANES, (g + 1) * LANES)
            wbd_s[cols, cols] = w_ref[g].astype(BF16)

    zero_blk = jnp.zeros((POOL_HALO, D_BRANCH), BF16)
    for s in (hi_s, lo_s):
        for s_i in range(ns):
            s[s_i * slot:s_i * slot + POOL_HALO, :] = zero_blk
            s[(s_i + 1) * slot - POOL_HALO:(s_i + 1) * slot, :] = zero_blk

    def split_body(c, carry):
        r0, w0, _ = locate(c)
        dst = pl.ds(pl.multiple_of(w0 + POOL_HALO, POOL_HALO), CHUNK)
        hi, lo = _split_bf16(p_ref[pl.ds(r0, CHUNK), 0:D_BRANCH], 2)
        hi_s[dst, :] = hi
        lo_s[dst, :] = lo
        return carry

    lax.fori_loop(0, ns * nb, split_body, 0)

    def one_chunk(c):
        r0, w0, t0 = locate(c)
        rows = pl.ds(r0, CHUNK)
        win = pl.ds(w0, POOL_WIN)
        t = t0 + lax.broadcasted_iota(jnp.int32, (CHUNK, 1), 0)
        means = []
        for g, w in enumerate(POOL_WINDOWS):
            cols = slice(g * LANES, (g + 1) * LANES)
            parts = jnp.dot(band_s[g], jnp.concatenate([hi_s[win, cols], lo_s[win, cols]], axis=1),
                            preferred_element_type=F32)
            lo = jnp.clip(t - w // 2, 0, seq)
            hi = jnp.clip(t - w // 2 + w, 0, seq)
            means.append((parts[:, 0:LANES] + parts[:, LANES:2 * LANES]) / (hi - lo).astype(F32))
        pooled = jnp.concatenate(means, axis=1) - p_ref[rows, 0:D_BRANCH]
        out = jnp.dot(pooled.astype(BF16), wbd_s[...], preferred_element_type=F32) + b_ref[...]
        gate = p_ref[rows, D_BRANCH:2 * D_BRANCH]
        o_ref[rows, :] = (out * sc_ref[...] * _silu(gate)).astype(BF16)

    def blk_body(c2, carry):
        for u in range(POOL_UNROLL):
            one_chunk(c2 * POOL_UNROLL + u)
        return carry

    lax.fori_loop(0, ns * nb // POOL_UNROLL, blk_body, 0)


def _pool(proj, lp, *, n_seq, seq):
    ns = max(1, DEC_SEQ // seq)
    return pl.pallas_call(
        functools.partial(_pool_kernel, seq=seq, ns=ns),
        grid=(n_seq // ns,),
        in_specs=[
            pl.BlockSpec((ns * seq, 2 * D_BRANCH), lambda b: (b, 3)),
            pl.BlockSpec((4, LANES, LANES), lambda b: (0, 0, 0)),
            pl.BlockSpec((1, D_BRANCH), lambda b: (0, 0)),
            pl.BlockSpec((1, D_BRANCH), lambda b: (0, 0)),
        ],
        out_specs=pl.BlockSpec((ns * seq, D_BRANCH), lambda b: (b, 0)),
        out_shape=jax.ShapeDtypeStruct((n_seq * seq, D_BRANCH), BF16),
        scratch_shapes=[
            pltpu.VMEM((ns * (seq + 2 * POOL_HALO), D_BRANCH), BF16),
            pltpu.VMEM((ns * (seq + 2 * POOL_HALO), D_BRANCH), BF16),
            pltpu.VMEM((len(POOL_WINDOWS), CHUNK, POOL_WIN), BF16),
            pltpu.VMEM((D_BRANCH, D_BRANCH), BF16),
        ],
        compiler_params=pltpu.CompilerParams(vmem_limit_bytes=VMEM_LIMIT),
        name="pool",
    )(proj, lp["pool_w"], lp["pool_b"], lp["pool_scale"])


def _rope(x, cos, sin):
    lane = lax.broadcasted_iota(jnp.int32, x.shape, 1)
    first = (lane & 63) < 32
    swapped = jnp.where(first, pltpu.roll(x, 96, axis=1), pltpu.roll(x, 32, axis=1))
    return x * cos + swapped * sin


def _load_cache(c_ref, c_s, n_heads):
    for h in range(n_heads):
        c_s[:, h * HEAD_DIM:(h + 1) * HEAD_DIM] = c_ref[0, 0, pl.ds(h, PAST_LEN, stride=n_heads), :].astype(BF16)


def _win_attn_kernel(sink_ref, p_ref, kc_ref, vc_ref, cos_ref, sin_ref, o_ref, q_s, k_s, v_s, kc_s, vc_s):
    seq = DEC_SEQ
    nb = seq // CHUNK
    kv_w = A_KV_HEADS * HEAD_DIM
    k_off = A_HEADS * HEAD_DIM
    v_off = k_off + kv_w
    g_off = v_off + kv_w
    zero_blk = jnp.zeros((CHUNK, kv_w), BF16)
    for s in (k_s, v_s):
        s[0:CHUNK, :] = zero_blk
        s[seq + CHUNK:seq + 2 * CHUNK, :] = zero_blk
    _load_cache(kc_ref, kc_s, A_KV_HEADS)
    _load_cache(vc_ref, vc_s, A_KV_HEADS)

    def prep_body(c, carry):
        r0 = pl.multiple_of(c * CHUNK, CHUNK)
        rows = pl.ds(r0, CHUNK)
        prow = pl.ds(pl.multiple_of(r0 + CHUNK, CHUNK), CHUNK)
        cos = cos_ref[rows, :]
        sin = sin_ref[rows, :]
        for h in range(A_HEADS):
            cols = slice(h * HEAD_DIM, (h + 1) * HEAD_DIM)
            q_s[rows, cols] = _rope(p_ref[rows, cols], cos, sin).astype(BF16)
        for kk in range(A_KV_HEADS):
            cols = slice(kk * HEAD_DIM, (kk + 1) * HEAD_DIM)
            k_s[prow, cols] = _rope(p_ref[rows, k_off + kk * HEAD_DIM:k_off + (kk + 1) * HEAD_DIM], cos, sin).astype(BF16)
        v_s[prow, :] = p_ref[rows, v_off:v_off + kv_w].astype(BF16)
        return carry

    lax.fori_loop(0, nb, prep_body, 0)

    grp = A_HEADS // A_KV_HEADS

    def blk_body(n, carry):
        r0 = pl.multiple_of(n * CHUNK, CHUNK)
        rows = pl.ds(r0, CHUNK)
        win = pl.ds(r0, 3 * CHUNK)
        ii = lax.broadcasted_iota(jnp.int32, (grp * CHUNK, 3 * CHUNK), 0)
        jj = lax.broadcasted_iota(jnp.int32, (grp * CHUNK, 3 * CHUNK), 1)
        rel = jj - CHUNK - (ii & (CHUNK - 1))
        kpos = r0 - CHUNK + jj
        valid = (rel >= -A_WINDOW) & (rel <= A_WINDOW) & (kpos >= 0) & (kpos < seq)
        first_head = lax.broadcasted_iota(jnp.int32, (grp * CHUNK, 1), 0) < CHUNK
        for kk in range(A_KV_HEADS):
            kcols = slice(kk * HEAD_DIM, (kk + 1) * HEAD_DIM)
            h0 = kk * grp
            q = jnp.concatenate([q_s[rows, (h0 + g) * HEAD_DIM:(h0 + g + 1) * HEAD_DIM] for g in range(grp)], axis=0)
            s_loc = lax.dot_general(q, k_s[win, kcols], TRANS_B, preferred_element_type=F32)
            s_loc = jnp.where(valid, s_loc, NEG_INF)
            s_ctx = lax.dot_general(q, kc_s[:, kcols], TRANS_B, preferred_element_type=F32)
            sink = jnp.where(first_head, sink_ref[h0], sink_ref[h0 + 1])
            o = _softmax_av([s_loc, s_ctx], [v_s[win, kcols], vc_s[:, kcols]], sink)
            for g in range(grp):
                cols = slice((h0 + g) * HEAD_DIM, (h0 + g + 1) * HEAD_DIM)
                gate = p_ref[rows, g_off + (h0 + g) * HEAD_DIM:g_off + (h0 + g + 1) * HEAD_DIM]
                o_ref[rows, cols] = (o[g * CHUNK:(g + 1) * CHUNK, :] * _silu(gate)).astype(BF16)
        return carry

    lax.fori_loop(0, nb, blk_body, 0)


def _win_attn(proj, sink, kc, vc, cos_t, sin_t, layer):
    seq = DEC_SEQ
    kv_w = A_KV_HEADS * HEAD_DIM
    cache_spec = pl.BlockSpec((1, 1, PAST_LEN * A_KV_HEADS, HEAD_DIM), lambda b: (b, layer, 0, 0))
    tab_spec = pl.BlockSpec((seq, HEAD_DIM), lambda b: (0, 0))
    return pl.pallas_call(
        _win_attn_kernel,
        grid=(DEC_BATCH,),
        in_specs=[
            pl.BlockSpec(memory_space=pltpu.SMEM),
            pl.BlockSpec((seq, 3 * D_BRANCH), lambda b: (b, 0)),
            cache_spec, cache_spec, tab_spec, tab_spec,
        ],
        out_specs=pl.BlockSpec((seq, D_BRANCH), lambda b: (b, 0)),
        out_shape=jax.ShapeDtypeStruct((DEC_BATCH * seq, D_BRANCH), BF16),
        scratch_shapes=[
            pltpu.VMEM((seq, A_HEADS * HEAD_DIM), BF16),
            pltpu.VMEM((seq + 2 * CHUNK, kv_w), BF16),
            pltpu.VMEM((seq + 2 * CHUNK, kv_w), BF16),
            pltpu.VMEM((PAST_LEN, kv_w), BF16),
            pltpu.VMEM((PAST_LEN, kv_w), BF16),
        ],
        compiler_params=pltpu.CompilerParams(vmem_limit_bytes=VMEM_LIMIT),
        name="win_attn",
    )(sink, proj, kc, vc, cos_t, sin_t)


NA_ROWS = DEC_SEQ // GRID_W
NA_KROWS = min(NA_KH, NA_ROWS)
NA_PAIRS = 2 * NA_KH - 2
NA_QROWS = 4
NA_WROWS = NA_QROWS + NA_KROWS
NA_GROUPS = NA_ROWS // NA_QROWS


def _na_row_start(r):
    return min(max(r - NA_KROWS // 2, 0), NA_ROWS - NA_KROWS)


def _na_win_start(qb):
    return min(max(qb * NA_QROWS - NA_KROWS // 2, 0), NA_ROWS - NA_WROWS)


def _na_bias_kernel(rpb_ref, o_ref, pair_s):
    layer = pl.program_id(0)
    h = pl.program_id(1)
    n_dy = 2 * NA_KH - 1
    n_dx = 2 * NA_KW - 1
    base = (layer * D_HEADS + h) * (n_dy * n_dx)
    qc = lax.broadcasted_iota(jnp.int32, (GRID_W, 2 * GRID_W), 0)
    lane = lax.broadcasted_iota(jnp.int32, (GRID_W, 2 * GRID_W), 1)
    second = lane >= GRID_W
    kc = lane & (GRID_W - 1)
    idx = jnp.clip(kc - qc, -(NA_KW - 1), NA_KW - 1) + (NA_KW - 1)
    cs = jnp.clip(qc - NA_KW // 2, 0, GRID_W - NA_KW)
    col_ok = (kc >= cs) & (kc < cs + NA_KW)
    for e in range(NA_PAIRS):
        val = jnp.zeros((GRID_W, 2 * GRID_W), F32)
        for d in range(n_dx):
            r0 = rpb_ref[base + e * n_dx + d]
            r1 = rpb_ref[base + (e + 1) * n_dx + d]
            val = jnp.where(idx == d, jnp.where(second, r1, r0), val)
        pair_s[e] = jnp.where(col_ok, val * (1.0 / ATT_SCALE), NEG_INF)

    masked = jnp.full((GRID_W, 2 * GRID_W), NEG_INF, F32)
    for qb in range(NA_GROUPS):
        ws = _na_win_start(qb)
        for ri in range(NA_QROWS):
            r = qb * NA_QROWS + ri
            rs = _na_row_start(r)
            assert ws <= rs and rs + NA_KROWS <= ws + NA_WROWS
            for p in range(NA_WROWS // 2):
                kr0 = ws + 2 * p
                ok0 = rs <= kr0 < rs + NA_KROWS
                ok1 = rs <= kr0 + 1 < rs + NA_KROWS
                e = kr0 - r + (NA_KH - 1)
                if ok0 or ok1:
                    assert 0 <= e < NA_PAIRS
                    blk = pair_s[e]
                    if not ok1:
                        blk = jnp.where(second, NEG_INF, blk)
                    if not ok0:
                        blk = jnp.where(second, blk, NEG_INF)
                else:
                    blk = masked
                o_ref[0, 0, qb, ri * GRID_W:(ri + 1) * GRID_W, p * 2 * GRID_W:(p + 1) * 2 * GRID_W] = blk


def _na_bias(na_rpb):
    blk = (NA_GROUPS, NA_QROWS * GRID_W, NA_WROWS * GRID_W)
    return pl.pallas_call(
        _na_bias_kernel,
        grid=(DEPTH, D_HEADS),
        in_specs=[pl.BlockSpec(memory_space=pltpu.SMEM)],
        out_specs=pl.BlockSpec((1, 1) + blk, lambda l, h: (l, h, 0, 0, 0)),
        out_shape=jax.ShapeDtypeStruct((DEPTH, D_HEADS) + blk, F32),
        scratch_shapes=[pltpu.VMEM((NA_PAIRS, GRID_W, 2 * GRID_W), F32)],
        name="na_bias",
    )(na_rpb.reshape(-1))


def _na_kernel(q_ref, k_ref, v_ref, g_ref, kc_ref, vc_ref, tab_ref, o_ref, k_s, v_s, kc_s, vc_s):
    h = pl.program_id(1)

    @pl.when(h == 0)
    def _():
        for hh in range(D_HEADS):
            kc_s[hh] = kc_ref[0, 0, pl.ds(hh, PAST_LEN, stride=D_HEADS), :].astype(BF16)
            vc_s[hh] = vc_ref[0, 0, pl.ds(hh, PAST_LEN, stride=D_HEADS), :].astype(BF16)

    k_s[...] = k_ref[...].astype(BF16)
    v_s[...] = v_ref[...].astype(BF16)
    kc = kc_s[h]
    vc = vc_s[h]
    nq = NA_QROWS * GRID_W
    for qb in range(NA_GROUPS):
        rows = slice(qb * nq, (qb + 1) * nq)
        ws = _na_win_start(qb)
        krows = slice(ws * GRID_W, (ws + NA_WROWS) * GRID_W)
        q = q_ref[rows, :].astype(BF16)
        s_loc = lax.dot_general(q, k_s[krows, :], TRANS_B, preferred_element_type=F32) + tab_ref[0, 0, qb]
        s_ctx = lax.dot_general(q, kc, TRANS_B, preferred_element_type=F32)
        o = _softmax_av([s_loc, s_ctx], [v_s[krows, :], vc])
        o_ref[rows, :] = (o * _silu(g_ref[rows, :])).astype(BF16)


def _na_attn(proj, kc, vc, tab, layer):
    seq = DEC_SEQ
    first = (D_MAIN - 4 * D_HEADS * HEAD_DIM) // HEAD_DIM
    col_spec = lambda g: pl.BlockSpec((seq, HEAD_DIM), lambda b, h: (b, first + g * D_HEADS + h))
    cache_spec = pl.BlockSpec((1, 1, PAST_LEN * D_HEADS, HEAD_DIM), lambda b, h: (b, layer, 0, 0))
    tab_blk = (NA_GROUPS, NA_QROWS * GRID_W, NA_WROWS * GRID_W)
    return pl.pallas_call(
        _na_kernel,
        grid=(DEC_BATCH, D_HEADS),
        in_specs=[
            col_spec(0), col_spec(1), col_spec(2), col_spec(3),
            cache_spec, cache_spec,
            pl.BlockSpec((1, 1) + tab_blk, lambda b, h: (layer, h, 0, 0, 0)),
        ],
        out_specs=pl.BlockSpec((seq, HEAD_DIM), lambda b, h: (b, h)),
        out_shape=jax.ShapeDtypeStruct((DEC_BATCH * seq, D_HEADS * HEAD_DIM), BF16),
        scratch_shapes=[pltpu.VMEM((seq, HEAD_DIM), BF16)] * 2 + [pltpu.VMEM((D_HEADS, PAST_LEN, HEAD_DIM), BF16)] * 2,
        compiler_params=pltpu.CompilerParams(
            dimension_semantics=("arbitrary", "arbitrary"), vmem_limit_bytes=VMEM_LIMIT),
        name="na_attn",
    )(proj, proj, proj, proj, kc, vc, tab)


def _rope_tables(seq):
    t = jnp.arange(seq)
    rows = (t // GRID_W).astype(F32)
    cols = (t % GRID_W).astype(F32)
    nf = HEAD_DIM // 4
    inv = ROPE_THETA ** (-jnp.arange(nf, dtype=F32) / nf)
    ar = rows[:, None] * inv[None, :]
    ac = cols[:, None] * inv[None, :]
    cos_t = jnp.concatenate([jnp.cos(ar), jnp.cos(ar), jnp.cos(ac), jnp.cos(ac)], axis=-1)
    sin_t = jnp.concatenate([-jnp.sin(ar), jnp.sin(ar), -jnp.sin(ac), jnp.sin(ac)], axis=-1)
    return cos_t, sin_t


def _pad_lanes(v, width=LANES):
    v = v.reshape(1, -1)
    return jnp.pad(v, ((0, 0), (0, width - v.shape[1])))


def kernel(x_prompt, x_sample, cache_attn_k, cache_attn_v, cache_na_k, cache_na_v, state_ssm_fwd, state_ssm_bwd, c, c_ctx, w_ada, b_ada, w_in, w_out, ln_g, ln_b, attn_sink, ssm_conv_w, ssm_conv_b, ssm_a_log, ssm_dt_bias, ssm_d, ssm_norm_w, pool_w, pool_b, pool_scale, na_rpb):
    alpha = (2.0 * DEPTH) ** 0.25
    n_x = B_HEADS * B_HEAD_DIM
    xc = x_prompt.reshape(BATCH * SEQ, D_MODEL)
    xl = x_sample.reshape(DEC_BATCH * DEC_SEQ, D_MODEL)

    cv = jnp.zeros((16, D_MODEL), F32).at[0].set(c_ctx).at[1:1 + DEC_BATCH].set(c)
    mod = _ada(cv, w_ada, b_ada)
    cos_t, sin_t = _rope_tables(DEC_SEQ)
    na_tab = _na_bias(na_rpb)

    kc_a = cache_attn_k.reshape(DEC_BATCH, DEPTH, PAST_LEN * A_KV_HEADS, HEAD_DIM)
    vc_a = cache_attn_v.reshape(DEC_BATCH, DEPTH, PAST_LEN * A_KV_HEADS, HEAD_DIM)
    kc_d = cache_na_k.reshape(DEC_BATCH, DEPTH, PAST_LEN * D_HEADS, HEAD_DIM)
    vc_d = cache_na_v.reshape(DEC_BATCH, DEPTH, PAST_LEN * D_HEADS, HEAD_DIM)
    h0 = (state_ssm_fwd.reshape(DEC_BATCH, DEPTH, n_x, B_STATE), state_ssm_bwd.reshape(DEC_BATCH, DEPTH, n_x, B_STATE))

    ctx_row = lambda i: 0
    lat_in_row = lambda i: 1 + i * IN_TM // DEC_SEQ
    lat_out_row = lambda i: 1 + i * OUT_TM // DEC_SEQ

    w_main, w_dt = _prep_w(w_in)
    w_o = w_out.astype(BF16)
    kv_a = kv_d = ssm_state = None
    for l in range(DEPTH):
        mod3 = mod[l].reshape(16, 1, 3 * D_MODEL)
        lp = {
            "conv_w": jnp.pad(ssm_conv_w[l], ((0, ROW_PAD - B_CONV), (0, 0))),
            "conv_b": ssm_conv_b[l].reshape(1, -1),
            "dt_bias": _pad_lanes(ssm_dt_bias[l]),
            "a_log": _pad_lanes(ssm_a_log[l]),
            "d_skip": jnp.repeat(ssm_d[l], B_HEAD_DIM).reshape(1, n_x),
            "norm_w": ssm_norm_w[l].reshape(1, n_x),
            "pool_w": pool_w[l],
            "pool_b": pool_b[l].reshape(1, D_BRANCH),
            "pool_scale": pool_scale[l].reshape(1, D_BRANCH),
        }
        sink = attn_sink[l]

        proj_c, dt_c = _inproj(xc, mod3, w_main, w_dt, ctx_row, l)
        o_a, *kv_a = _ctx_attn(proj_c, sink, kv_a, l, n_q=A_HEADS, n_kv=A_KV_HEADS, col_block=0,
                               width=3 * D_BRANCH, use_sink=True)
        o_b, *ssm_state = _ssm(proj_c, dt_c, lp, l, n_seq=BATCH, seq=SEQ, prev_state=ssm_state)
        o_c = _pool(proj_c, lp, n_seq=BATCH, seq=SEQ)
        o_d, *kv_d = _ctx_attn(proj_c, sink, kv_d, l, n_q=D_HEADS, n_kv=D_HEADS, col_block=2,
                               width=4 * D_BRANCH, use_sink=False)
        xc = _outproj((o_a, o_b, o_c, o_d), xc, mod3, w_o, ln_g[l], ln_b[l], ctx_row, alpha, l)

        proj_l, dt_l = _inproj(xl, mod3, w_main, w_dt, lat_in_row, l)
        o_a = _win_attn(proj_l, sink, kc_a, vc_a, cos_t, sin_t, l)
        o_b = _ssm(proj_l, dt_l, lp, l, n_seq=DEC_BATCH, seq=DEC_SEQ, h0=h0)[0]
        o_c = _pool(proj_l, lp, n_seq=DEC_BATCH, seq=DEC_SEQ)
        o_d = _na_attn(proj_l, kc_d, vc_d, na_tab, l)
        xl = _outproj((o_a, o_b, o_c, o_d), xl, mod3, w_o, ln_g[l], ln_b[l], lat_out_row, alpha, l)

    return (
        xc.reshape(BATCH, SEQ, D_MODEL),
        xl.reshape(DEC_BATCH, DEC_SEQ, D_MODEL),
        kv_a[0].reshape(BATCH, DEPTH, SEQ, A_KV_HEADS, HEAD_DIM),
        kv_a[1].reshape(BATCH, DEPTH, SEQ, A_KV_HEADS, HEAD_DIM),
        kv_d[0].reshape(BATCH, DEPTH, SEQ, D_HEADS, HEAD_DIM),
        kv_d[1].reshape(BATCH, DEPTH, SEQ, D_HEADS, HEAD_DIM),
        ssm_state[0].reshape(BATCH, DEPTH, B_HEADS, B_HEAD_DIM, B_STATE),
        ssm_state[1].reshape(BATCH, DEPTH, B_HEADS, B_HEAD_DIM, B_STATE),
    )
```

```python
import functools

import jax
import jax.numpy as jnp
from jax import lax
from jax.experimental import pallas as pl
from jax.experimental.pallas import tpu as pltpu

F32 = jnp.float32
BF16 = jnp.bfloat16

D_MODEL = 2048
BATCH = 32
SEQ = 256
DEPTH = 2
DEC_BATCH = 4
DEC_SEQ = 1024
PAST_LEN = 256
GRID_W = 64
D_BRANCH = 512
HEAD_DIM = 128
A_HEADS = 4
A_KV_HEADS = 2
A_WINDOW = 128
ROPE_THETA = 10000.0
B_HEADS = 8
B_HEAD_DIM = 64
B_STATE = 128
B_CONV = 5
CHUNK = 128
POOL_WINDOWS = (2, 4, 8, 16)
D_HEADS = 4
NA_KH = 8
NA_KW = 16
LN_EPS = 1e-6
NEG_INF = -1e30
ATT_SCALE = HEAD_DIM ** -0.5
LOG2E = 1.4426950408889634
EXP2_SCALE = ATT_SCALE * LOG2E

D_MAIN = 6144
DT_OFF = 3072
DT_COLS = 16
LANES = 128
ROW_PAD = 8
VMEM_LIMIT = 56 * 1024 * 1024

TRANS_B = (((1,), (1,)), ((), ()))


def _silu(x):
    hx = 0.5 * x
    return hx + hx * jnp.tanh(hx)


def _split_bf16(a, parts):
    out = []
    rem = a
    for _ in range(parts):
        hi = rem.astype(BF16)
        out.append(hi)
        rem = rem - hi.astype(F32)
    return out


def _dot01_lhs(m01, a, parts=3):
    acc = None
    for p in _split_bf16(a, parts):
        t = jnp.dot(m01, p, preferred_element_type=F32)
        acc = t if acc is None else acc + t
    return acc


def _dot01_rhs(a, m01, parts=2):
    acc = None
    for p in _split_bf16(a, parts):
        t = jnp.dot(p, m01, preferred_element_type=F32)
        acc = t if acc is None else acc + t
    return acc


def _scaled_query(q):
    return (q * EXP2_SCALE).astype(BF16)


def _softmax_av(scores, values, sink=None):
    m = None
    for s in scores:
        bm = jnp.max(s, axis=-1, keepdims=True)
        m = bm if m is None else jnp.maximum(m, bm)
    if sink is not None:
        sink = sink * LOG2E
        m = jnp.maximum(m, sink)
    den = None
    acc = None
    for s, v in zip(scores, values):
        p = jnp.exp2(s - m)
        d = jnp.sum(p, axis=-1, keepdims=True)
        o = jnp.dot(p.astype(BF16), v, preferred_element_type=F32)
        den = d if den is None else den + d
        acc = o if acc is None else acc + o
    if sink is not None:
        den = den + jnp.exp2(sink - m)
    return acc / den


def _ada_kernel(cv_ref, w_ref, b_ref, o_ref):
    a = _silu(cv_ref[...]).astype(BF16)
    w = w_ref[0].astype(BF16)
    o_ref[0] = jnp.dot(a, w, preferred_element_type=F32) + b_ref[0]


def _ada(cv, w_ada, b_ada):
    tn = 1024
    n = 3 * D_MODEL
    return pl.pallas_call(
        _ada_kernel,
        grid=(DEPTH, n // tn),
        in_specs=[
            pl.BlockSpec((16, D_MODEL), lambda l, j: (0, 0)),
            pl.BlockSpec((1, D_MODEL, tn), lambda l, j: (l, 0, j)),
            pl.BlockSpec((1, 1, tn), lambda l, j: (l, 0, j)),
        ],
        out_specs=pl.BlockSpec((1, 16, tn), lambda l, j: (l, 0, j)),
        out_shape=jax.ShapeDtypeStruct((DEPTH, 16, n), F32),
        compiler_params=pltpu.CompilerParams(vmem_limit_bytes=VMEM_LIMIT),
        name="ada_mod",
    )(cv, w_ada, b_ada.reshape(DEPTH, 1, n))


PREP_TN = 512
PREP_KC = 256
N_LOW = DT_OFF // PREP_TN


def _prep_w_kernel(a_ref, b_ref, c_ref, o_ref, odt_ref):
    j = pl.program_id(1)
    kcs = [slice(k * PREP_KC, (k + 1) * PREP_KC) for k in range(D_MODEL // PREP_KC)]

    @pl.when(j == 0)
    def _():
        row = lax.broadcasted_iota(jnp.int32, (LANES, PREP_KC), 0)
        for ks in kcs:
            odt_ref[0, ks, :] = jnp.where(row < DT_COLS, c_ref[0, :, ks], 0.0).T.astype(BF16)

    @pl.when(j < N_LOW)
    def _():
        for ks in kcs:
            o_ref[0, ks, :] = a_ref[0, :, ks].T.astype(BF16)

    @pl.when(j >= N_LOW)
    def _():
        for ks in kcs:
            src = jnp.concatenate([a_ref[0, DT_COLS:, ks], b_ref[0, :, ks]], axis=0)
            o_ref[0, ks, :] = src.T.astype(BF16)


def _prep_w(w_in):
    w_t = jnp.swapaxes(w_in, 1, 2)
    tail_blocks = PREP_TN // DT_COLS
    return pl.pallas_call(
        _prep_w_kernel,
        grid=(DEPTH, D_MAIN // PREP_TN),
        in_specs=[
            pl.BlockSpec((1, PREP_TN, D_MODEL), lambda l, j: (l, j, 0)),
            pl.BlockSpec((1, DT_COLS, D_MODEL), lambda l, j: (l, jnp.where(j >= N_LOW, (j + 1) * tail_blocks, 0), 0)),
            pl.BlockSpec((1, LANES, D_MODEL), lambda l, j: (l, DT_OFF // LANES, 0)),
        ],
        out_specs=[
            pl.BlockSpec((1, D_MODEL, PREP_TN), lambda l, j: (l, 0, j)),
            pl.BlockSpec((1, D_MODEL, LANES), lambda l, j: (l, 0, 0)),
        ],
        out_shape=[
            jax.ShapeDtypeStruct((DEPTH, D_MODEL, D_MAIN), BF16),
            jax.ShapeDtypeStruct((DEPTH, D_MODEL, LANES), BF16),
        ],
        compiler_params=pltpu.CompilerParams(vmem_limit_bytes=VMEM_LIMIT),
        name="prep_w",
    )(w_t, w_t, w_t)


IN_TM = 1024
IN_TN = 768
IN_NJ = D_MAIN // IN_TN
IN_RC = IN_TM // IN_NJ


LN_SLAB = 16


def _ln_modulate(x_ref, x_row0, u_ref, u_row0, n_rows, mod_ref):
    shift = mod_ref[0, :, 0:D_MODEL]
    scale1 = 1.0 + mod_ref[0, :, D_MODEL:2 * D_MODEL]
    for s in range(n_rows // LN_SLAB):
        xf = x_ref[pl.ds(x_row0 + s * LN_SLAB, LN_SLAB), :]
        mu = jnp.mean(xf, axis=-1, keepdims=True)
        xc = xf - mu
        var = jnp.mean(xc * xc, axis=-1, keepdims=True)
        u = xc * lax.rsqrt(var + LN_EPS) * scale1 + shift
        u_ref[pl.ds(u_row0 + s * LN_SLAB, LN_SLAB), :] = u.astype(BF16)


def _inproj_kernel(x0_ref, xn_ref, mod0_ref, modn_ref, w_ref, wdt_ref, o_ref, dt_ref, u_a, u_b):
    i = pl.program_id(0)
    j = pl.program_id(1)

    @pl.when((i == 0) & (j == 0))
    def _():
        def body(r, carry):
            r0 = pl.multiple_of(r * IN_RC, IN_RC)
            _ln_modulate(x0_ref, r0, u_a, r0, IN_RC, mod0_ref)
            return carry

        lax.fori_loop(0, IN_NJ, body, 0)

    row_j = pl.multiple_of(j * IN_RC, IN_RC)

    def step(cur, nxt):
        o_ref[...] = jnp.dot(cur[...], w_ref[0], preferred_element_type=F32)
        dt_ref[...] = jnp.dot(cur[pl.ds(row_j, IN_RC), :], wdt_ref[0], preferred_element_type=F32)
        _ln_modulate(xn_ref, 0, nxt, row_j, IN_RC, modn_ref)

    parity = lax.rem(i, 2)

    @pl.when(parity == 0)
    def _():
        step(u_a, u_b)

    @pl.when(parity == 1)
    def _():
        step(u_b, u_a)


def _inproj(x2d, mod3, w_main, w_dt, mod_row, layer):
    m = x2d.shape[0]
    n_i = m // IN_TM
    nxt = lambda i: jnp.minimum(i + 1, n_i - 1)
    return pl.pallas_call(
        _inproj_kernel,
        grid=(n_i, IN_NJ),
        in_specs=[
            pl.BlockSpec((IN_TM, D_MODEL), lambda i, j: (0, 0)),
            pl.BlockSpec((IN_RC, D_MODEL), lambda i, j: (nxt(i) * IN_NJ + j, 0)),
            pl.BlockSpec((1, 1, 3 * D_MODEL), lambda i, j: (mod_row(0), 0, 0)),
            pl.BlockSpec((1, 1, 3 * D_MODEL), lambda i, j: (mod_row(nxt(i)), 0, 0)),
            pl.BlockSpec((1, D_MODEL, IN_TN), lambda i, j: (layer, 0, j)),
            pl.BlockSpec((1, D_MODEL, LANES), lambda i, j: (layer, 0, 0)),
        ],
        out_specs=[
            pl.BlockSpec((IN_TM, IN_TN), lambda i, j: (i, j)),
            pl.BlockSpec((IN_RC, LANES), lambda i, j: (i * IN_NJ + j, 0)),
        ],
        out_shape=[
            jax.ShapeDtypeStruct((m, D_MAIN), F32),
            jax.ShapeDtypeStruct((m, LANES), F32),
        ],
        scratch_shapes=[pltpu.VMEM((IN_TM, D_MODEL), BF16)] * 2,
        compiler_params=pltpu.CompilerParams(
            dimension_semantics=("arbitrary", "arbitrary"), vmem_limit_bytes=VMEM_LIMIT),
        name="in_proj",
    )(x2d, x2d, mod3, mod3, w_main, w_dt)


OUT_TM = 512
OUT_RC = 128


def _outproj_kernel(ma_ref, mb_ref, mc_ref, md_ref, x_ref, mod_ref, w_ref, g_ref, b_ref, o_ref, acc_ref, *, alpha):
    gate = mod_ref[0, :, 2 * D_MODEL:3 * D_MODEL]
    mixed = jnp.concatenate([ma_ref[...], mb_ref[...], mc_ref[...], md_ref[...]], axis=1)
    acc_ref[...] = jnp.dot(mixed, w_ref[0], preferred_element_type=F32)

    def body(r, carry):
        rows = pl.ds(pl.multiple_of(r * OUT_RC, OUT_RC), OUT_RC)
        z = alpha * x_ref[rows, :] + acc_ref[rows, :] * gate
        mu = jnp.mean(z, axis=-1, keepdims=True)
        zc = z - mu
        var = jnp.mean(zc * zc, axis=-1, keepdims=True)
        o_ref[rows, :] = zc * lax.rsqrt(var + LN_EPS) * g_ref[...] + b_ref[...]
        return carry

    lax.fori_loop(0, OUT_TM // OUT_RC, body, 0)


def _outproj(mixed, x2d, mod3, w_o, ln_g, ln_b, mod_row, alpha, layer):
    m = x2d.shape[0]
    mspec = pl.BlockSpec((OUT_TM, D_BRANCH), lambda i: (i, 0))
    return pl.pallas_call(
        functools.partial(_outproj_kernel, alpha=alpha),
        grid=(m // OUT_TM,),
        in_specs=[
            mspec, mspec, mspec, mspec,
            pl.BlockSpec((OUT_TM, D_MODEL), lambda i: (i, 0)),
            pl.BlockSpec((1, 1, 3 * D_MODEL), lambda i: (mod_row(i), 0, 0)),
            pl.BlockSpec((1, D_MODEL, D_MODEL), lambda i: (layer, 0, 0)),
            pl.BlockSpec((1, D_MODEL), lambda i: (0, 0)),
            pl.BlockSpec((1, D_MODEL), lambda i: (0, 0)),
        ],
        out_specs=pl.BlockSpec((OUT_TM, D_MODEL), lambda i: (i, 0)),
        out_shape=jax.ShapeDtypeStruct((m, D_MODEL), F32),
        scratch_shapes=[pltpu.VMEM((OUT_TM, D_MODEL), F32)],
        compiler_params=pltpu.CompilerParams(vmem_limit_bytes=VMEM_LIMIT),
        name="out_proj",
    )(*mixed, x2d, mod3, w_o, ln_g.reshape(1, D_MODEL), ln_b.reshape(1, D_MODEL))


CTX_NS = 2


def _ctx_attn_kernel(*refs, n_q, n_kv, use_sink, aliased):
    sink_ref, p_ref = refs[0], refs[1]
    o_ref, kn_ref, vn_ref = refs[4:7] if aliased else refs[2:5]
    grp = n_q // n_kv
    k_off = n_q * HEAD_DIM
    v_off = k_off + n_kv * HEAD_DIM
    g_off = v_off + n_kv * HEAD_DIM
    for s_i in range(CTX_NS):
        rows = slice(s_i * SEQ, (s_i + 1) * SEQ)
        for kk in range(n_kv):
            k32 = p_ref[rows, k_off + kk * HEAD_DIM:k_off + (kk + 1) * HEAD_DIM]
            v32 = p_ref[rows, v_off + kk * HEAD_DIM:v_off + (kk + 1) * HEAD_DIM]
            kn_ref[s_i, 0, pl.ds(kk, SEQ, stride=n_kv), :] = k32
            vn_ref[s_i, 0, pl.ds(kk, SEQ, stride=n_kv), :] = v32
            k = k32.astype(BF16)
            v = v32.astype(BF16)
            for g in range(grp):
                h = kk * grp + g
                cols = slice(h * HEAD_DIM, (h + 1) * HEAD_DIM)
                q = _scaled_query(p_ref[rows, cols])
                s = lax.dot_general(q, k, TRANS_B, preferred_element_type=F32)
                o = _softmax_av([s], [v], sink_ref[h] if use_sink else None)
                gate = p_ref[rows, g_off + h * HEAD_DIM:g_off + (h + 1) * HEAD_DIM]
                o_ref[rows, cols] = (o * _silu(gate)).astype(BF16)


def _ctx_attn(proj, sink, prev_kv, layer, *, n_q, n_kv, col_block, width, use_sink):
    aliased = prev_kv is not None
    kv_spec = pl.BlockSpec((CTX_NS, 1, SEQ * n_kv, HEAD_DIM), lambda b: (b, layer, 0, 0))
    kv_shape = jax.ShapeDtypeStruct((BATCH, DEPTH, SEQ * n_kv, HEAD_DIM), F32)
    in_specs = [
        pl.BlockSpec(memory_space=pltpu.SMEM),
        pl.BlockSpec((CTX_NS * SEQ, width), lambda b: (b, col_block)),
    ]
    args = [sink, proj]
    if aliased:
        in_specs += [pl.BlockSpec(memory_space=pl.ANY)] * 2
        args += list(prev_kv)
    return pl.pallas_call(
        functools.partial(_ctx_attn_kernel, n_q=n_q, n_kv=n_kv, use_sink=use_sink, aliased=aliased),
        grid=(BATCH // CTX_NS,),
        in_specs=in_specs,
        out_specs=[pl.BlockSpec((CTX_NS * SEQ, n_q * HEAD_DIM), lambda b: (b, 0)), kv_spec, kv_spec],
        out_shape=[jax.ShapeDtypeStruct((BATCH * SEQ, n_q * HEAD_DIM), BF16), kv_shape, kv_shape],
        input_output_aliases={2: 1, 3: 2} if aliased else {},
        compiler_params=pltpu.CompilerParams(vmem_limit_bytes=VMEM_LIMIT),
        name="ctx_attn",
    )(*args)


SSM_ROWS_PER_STEP = 512
CONV_WIN = CHUNK + 2 * ROW_PAD


class _SsmSeq:
    def __init__(self, s_i, seq, p_ref, dt_ref, o_ref, state_refs, scratch):
        rows = pl.ds(s_i * seq, seq)
        self.p = p_ref.at[rows, :]
        self.dt = dt_ref.at[rows, :]
        self.o = o_ref.at[rows, :]
        self.h0f, self.h0b, self.hf, self.hb = (None if r is None else r.at[s_i, 0] for r in state_refs)
        self.xpad, self.xbc_s, self.y_s, self.eb_s, self.sb_s, self.db_s, self.ht_s = (r.at[s_i] for r in scratch)


def _ssm_kernel(*refs, seq, nsq, has_h0, emit_state, aliased):
    nc = seq // CHUNK
    it = iter(refs)
    p_ref, dt_ref, cw_ref, cb_ref, dtb_ref, alog_ref, dskip_ref, nw_ref = (next(it) for _ in range(8))
    h0f_ref = next(it) if has_h0 else None
    h0b_ref = next(it) if has_h0 else None
    if aliased:
        next(it), next(it)
    o_ref = next(it)
    hf_ref = next(it) if emit_state else None
    hb_ref = next(it) if emit_state else None
    scratch = [next(it) for _ in range(7)]
    ex_s = next(it)
    seqs = [_SsmSeq(s_i, seq, p_ref, dt_ref, o_ref, (h0f_ref, h0b_ref, hf_ref, hb_ref), scratch)
            for s_i in range(nsq)]

    def for_each_seq(fn):
        def body(c, carry):
            for sq in seqs:
                fn(sq, c)
            return carry
        return body

    n_x = B_HEADS * B_HEAD_DIM
    n_xbc = 2 * n_x

    @pl.when(pl.program_id(0) == 0)
    def _():
        r_i = lax.broadcasted_iota(jnp.int32, (LANES, 2 * n_x), 0)
        c_i = lax.broadcasted_iota(jnp.int32, (LANES, 2 * n_x), 1)
        ex_s[...] = jnp.where(r_i == lax.shift_right_logical(c_i, 6), 1.0, 0.0).astype(BF16)

    zero_rows = jnp.zeros((ROW_PAD, n_xbc), F32)
    for sq in seqs:
        sq.xpad[0:ROW_PAD, :] = zero_rows
        sq.xpad[seq + ROW_PAD:seq + 2 * ROW_PAD, :] = zero_rows

    def copy_chunk(sq, c):
        r0 = pl.multiple_of(c * CHUNK, CHUNK)
        sq.xpad[pl.ds(pl.multiple_of(r0 + ROW_PAD, ROW_PAD), CHUNK), :] = sq.p[pl.ds(r0, CHUNK), 0:n_xbc]

    lax.fori_loop(0, nc, for_each_seq(copy_chunk), 0)

    def load_state(sq, h_view):
        for blk in range(4):
            cols = slice(blk * LANES, (blk + 1) * LANES)
            if h_view is None:
                sq.ht_s[:, cols] = jnp.zeros((B_STATE, LANES), F32)
            else:
                sq.ht_s[:, cols] = h_view[cols, :].T

    def store_state(sq, h_view):
        for blk in range(4):
            cols = slice(blk * LANES, (blk + 1) * LANES)
            h_view[cols, :] = sq.ht_s[:, cols].T

    for sq in seqs:
        load_state(sq, sq.h0f)

    def fwd_chunk(sq, c):
        r0 = pl.multiple_of(c * CHUNK, CHUNK)
        rows = pl.ds(r0, CHUNK)
        win = sq.xpad[pl.ds(r0, CONV_WIN), :]
        acc = jnp.broadcast_to(cb_ref[...], (CHUNK, n_xbc))
        for k in range(B_CONV):
            sh = (B_CONV // 2 - k) % CONV_WIN
            rolled = win if sh == 0 else pltpu.roll(win, sh, axis=0)
            acc = acc + cw_ref[k:k + 1, :] * rolled[ROW_PAD:ROW_PAD + CHUNK, :]
        xbc = _silu(acc)
        sq.xbc_s[rows, :] = xbc
        xs = xbc[:, 0:n_x]
        bm = xbc[:, n_x:n_x + 2 * B_STATE]
        cm = xbc[:, n_x + 2 * B_STATE:n_xbc]

        dtr = sq.dt[rows, :] + dtb_ref[...]
        dt = jnp.maximum(dtr, 0.0) + jnp.log1p(jnp.exp(-jnp.abs(dtr)))
        a = dt * (-jnp.exp(alog_ref[...]))
        ii = lax.broadcasted_iota(jnp.int32, (CHUNK, CHUNK), 0)
        jj = lax.broadcasted_iota(jnp.int32, (CHUNK, CHUNK), 1)
        tril = jj <= ii
        triu = jj >= ii
        tri = jnp.concatenate([jnp.where(tril, 1.0, 0.0), jnp.where(triu, 1.0, 0.0)], axis=0).astype(BF16)
        sums = jnp.dot(tri, jnp.concatenate(_split_bf16(a, 3), axis=1), preferred_element_type=F32)
        sums = sums[:, 0:LANES] + sums[:, LANES:2 * LANES] + sums[:, 2 * LANES:3 * LANES]
        fwd_lane = jj < B_HEADS
        lc = jnp.where(fwd_lane, sums[0:CHUNK, :], sums[CHUNK:2 * CHUNK, :])
        lend = jnp.where(fwd_lane[0:1, :], lc[CHUNK - 1:CHUNK, :], lc[0:1, :])
        toend = jnp.exp(lend - lc) * dt
        e_end = jnp.broadcast_to(jnp.exp(lend), (2 * ROW_PAD, LANES))
        stacked = jnp.concatenate(_split_bf16(jnp.exp(lc), 2) + _split_bf16(toend, 2) + _split_bf16(e_end, 2), axis=0)
        wide = jnp.dot(stacked, ex_s[...], preferred_element_type=F32)
        e_exp = wide[0:CHUNK, :] + wide[CHUNK:2 * CHUNK, :]
        w_exp = wide[2 * CHUNK:3 * CHUNK, :] + wide[3 * CHUNK:4 * CHUNK, :]
        d_exp = wide[4 * CHUNK:4 * CHUNK + ROW_PAD, :] + wide[4 * CHUNK + 2 * ROW_PAD:4 * CHUNK + 3 * ROW_PAD, :]
        lc_t = lc.T
        dt_t = dt.T
        sq.eb_s[rows, :] = e_exp[:, n_x:2 * n_x]
        sq.db_s[c] = d_exp[:, n_x:2 * n_x]

        lane_g = lax.broadcasted_iota(jnp.int32, (CHUNK, 2 * LANES), 1)
        for g in range(2):
            gcols = slice(g * 2 * LANES, (g + 1) * 2 * LANES)
            cg = cm[:, g * B_STATE:(g + 1) * B_STATE].astype(BF16)
            bg = bm[:, g * B_STATE:(g + 1) * B_STATE]
            cb = lax.dot_general(cg, bg.astype(BF16), TRANS_B, preferred_element_type=F32)
            bg_t = bg.T.astype(BF16)
            xg = xs[:, gcols]
            xblk = jnp.concatenate(
                [jnp.where((lane_g >= hh * B_HEAD_DIM) & (lane_g < (hh + 1) * B_HEAD_DIM), xg, 0.0).astype(BF16)
                 for hh in range(4)], axis=0)
            for d in range(2):
                mask = tril if d == 0 else triu
                ms = []
                for hh in range(4):
                    col = d * B_HEADS + g * 4 + hh
                    seg = lc[:, col:col + 1] - lc_t[col:col + 1, :]
                    dec = jnp.exp(jnp.where(mask, seg, -jnp.inf))
                    ms.append((cb * dec * dt_t[col:col + 1, :]).astype(BF16))
                y_in = jnp.dot(jnp.concatenate(ms, axis=1), xblk, preferred_element_type=F32)
                wg = w_exp[:, d * n_x + g * 2 * LANES:d * n_x + (g + 1) * 2 * LANES]
                st = jnp.dot(bg_t, (xg * wg).astype(BF16), preferred_element_type=F32)
                if d == 0:
                    h_prev = sq.ht_s[:, gcols]
                    y_x = e_exp[:, gcols] * jnp.dot(cg, h_prev.astype(BF16), preferred_element_type=F32)
                    sq.y_s[rows, gcols] = y_in + y_x
                    sq.ht_s[:, gcols] = d_exp[0:1, gcols] * h_prev + st
                else:
                    sq.y_s[rows, gcols] = sq.y_s[rows, gcols] + y_in
                    sq.sb_s[c, :, gcols] = st

    lax.fori_loop(0, nc, for_each_seq(fwd_chunk), 0)
    for sq in seqs:
        if emit_state:
            store_state(sq, sq.hf)
        load_state(sq, sq.h0b)

    def bwd_chunk(sq, i):
        c = nc - 1 - i
        r0 = pl.multiple_of(c * CHUNK, CHUNK)
        rows = pl.ds(r0, CHUNK)
        for g in range(2):
            gcols = slice(g * 2 * LANES, (g + 1) * 2 * LANES)
            cg = sq.xbc_s[rows, n_x + 2 * B_STATE + g * B_STATE:n_x + 2 * B_STATE + (g + 1) * B_STATE].astype(BF16)
            h_prev = sq.ht_s[:, gcols]
            y_x = sq.eb_s[rows, gcols] * jnp.dot(cg, h_prev.astype(BF16), preferred_element_type=F32)
            sq.y_s[rows, gcols] = sq.y_s[rows, gcols] + y_x
            sq.ht_s[:, gcols] = sq.db_s[c, 0:1, gcols] * h_prev + sq.sb_s[c, :, gcols]

    lax.fori_loop(0, nc, for_each_seq(bwd_chunk), 0)
    if emit_state:
        for sq in seqs:
            store_state(sq, sq.hb)

    def out_chunk(sq, c):
        rows = pl.ds(pl.multiple_of(c * CHUNK, CHUNK), CHUNK)
        y = sq.y_s[rows, :] + dskip_ref[...] * sq.xbc_s[rows, 0:n_x]
        yz = y * _silu(sq.p[rows, n_xbc:n_xbc + n_x])
        ms = jnp.mean(yz * yz, axis=-1, keepdims=True)
        sq.o[rows, :] = (yz * lax.rsqrt(ms + LN_EPS) * nw_ref[...]).astype(BF16)

    lax.fori_loop(0, nc, for_each_seq(out_chunk), 0)


def _ssm(proj, dt, lp, layer, *, n_seq, seq, h0=None, prev_state=None):
    has_h0 = h0 is not None
    emit_state = not has_h0
    aliased = prev_state is not None
    nc = seq // CHUNK
    nsq = max(1, SSM_ROWS_PER_STEP // seq)
    n_x = B_HEADS * B_HEAD_DIM
    const2 = lambda b: (0, 0)
    in_specs = [
        pl.BlockSpec((nsq * seq, 3 * n_x), lambda b: (b, 1)),
        pl.BlockSpec((nsq * seq, LANES), lambda b: (b, 0)),
        pl.BlockSpec((ROW_PAD, 2 * n_x), const2),
        pl.BlockSpec((1, 2 * n_x), const2),
        pl.BlockSpec((1, LANES), const2),
        pl.BlockSpec((1, LANES), const2),
        pl.BlockSpec((1, n_x), const2),
        pl.BlockSpec((1, n_x), const2),
    ]
    args = [proj, dt, lp["conv_w"], lp["conv_b"], lp["dt_bias"], lp["a_log"], lp["d_skip"], lp["norm_w"]]
    state_spec = pl.BlockSpec((nsq, 1, n_x, B_STATE), lambda b: (b, layer, 0, 0))
    if has_h0:
        in_specs += [state_spec, state_spec]
        args += list(h0)
    aliases = {}
    if aliased:
        aliases = {len(args): 1, len(args) + 1: 2}
        in_specs += [pl.BlockSpec(memory_space=pl.ANY)] * 2
        args += list(prev_state)
    out_specs = [pl.BlockSpec((nsq * seq, n_x), lambda b: (b, 0))]
    out_shape = [jax.ShapeDtypeStruct((n_seq * seq, n_x), BF16)]
    if emit_state:
        out_specs += [state_spec, state_spec]
        out_shape += [jax.ShapeDtypeStruct((n_seq, DEPTH, n_x, B_STATE), F32)] * 2
    return pl.pallas_call(
        functools.partial(_ssm_kernel, seq=seq, nsq=nsq, has_h0=has_h0, emit_state=emit_state, aliased=aliased),
        grid=(n_seq // nsq,),
        in_specs=in_specs,
        out_specs=out_specs,
        out_shape=out_shape,
        input_output_aliases=aliases,
        scratch_shapes=[
            pltpu.VMEM((nsq, seq + 2 * ROW_PAD, 2 * n_x), F32),
            pltpu.VMEM((nsq, seq, 2 * n_x), F32),
            pltpu.VMEM((nsq, seq, n_x), F32),
            pltpu.VMEM((nsq, seq, n_x), F32),
            pltpu.VMEM((nsq, nc, B_STATE, n_x), F32),
            pltpu.VMEM((nsq, nc, ROW_PAD, n_x), F32),
            pltpu.VMEM((nsq, B_STATE, n_x), F32),
            pltpu.VMEM((LANES, 2 * n_x), BF16),
        ],
        compiler_params=pltpu.CompilerParams(vmem_limit_bytes=VMEM_LIMIT),
        name="ssm",
    )(*args)


POOL_HALO = 16
POOL_WIN = CHUNK + 2 * POOL_HALO
POOL_UNROLL = 2


def _pool_kernel(p_ref, w_ref, b_ref, sc_ref, o_ref, hi_s, lo_s, band_s, wbd_s, *, seq, ns):
    nb = seq // CHUNK
    slot = seq + 2 * POOL_HALO

    def locate(c):
        s_i, cb = (0, c) if ns == 1 else (c // nb, c % nb)
        r0 = pl.multiple_of(c * CHUNK, CHUNK)
        return r0, pl.multiple_of(s_i * slot + cb * CHUNK, POOL_HALO), cb * CHUNK

    @pl.when(pl.program_id(0) == 0)
    def _():
        ii = lax.broadcasted_iota(jnp.int32, (CHUNK, POOL_WIN), 0)
        jj = lax.broadcasted_iota(jnp.int32, (CHUNK, POOL_WIN), 1)
        rel = jj - POOL_HALO - ii
        for g, w in enumerate(POOL_WINDOWS):
            band_s[g] = jnp.where((rel >= -(w // 2)) & (rel < w - w // 2), 1.0, 0.0).astype(BF16)
        wbd_s[...] = jnp.zeros((D_BRANCH, D_BRANCH), BF16)
        for g in range(len(POOL_WINDOWS)):
            cols = slice(g * LANES, (g + 1) * LANES)
            wbd_s[cols, cols] = w_ref[g].astype(BF16)

    zero_blk = jnp.zeros((POOL_HALO, D_BRANCH), BF16)
    for s in (hi_s, lo_s):
        for s_i in range(ns):
            s[s_i * slot:s_i * slot + POOL_HALO, :] = zero_blk
            s[(s_i + 1) * slot - POOL_HALO:(s_i + 1) * slot, :] = zero_blk

    def split_body(c, carry):
        r0, w0, _ = locate(c)
        dst = pl.ds(pl.multiple_of(w0 + POOL_HALO, POOL_HALO), CHUNK)
        hi, lo = _split_bf16(p_ref[pl.ds(r0, CHUNK), 0:D_BRANCH], 2)
        hi_s[dst, :] = hi
        lo_s[dst, :] = lo
        return carry

    lax.fori_loop(0, ns * nb, split_body, 0)

    def one_chunk(c):
        r0, w0, t0 = locate(c)
        rows = pl.ds(r0, CHUNK)
        win = pl.ds(w0, POOL_WIN)
        t = t0 + lax.broadcasted_iota(jnp.int32, (CHUNK, 1), 0)
        means = []
        for g, w in enumerate(POOL_WINDOWS):
            cols = slice(g * LANES, (g + 1) * LANES)
            parts = jnp.dot(band_s[g], jnp.concatenate([hi_s[win, cols], lo_s[win, cols]], axis=1),
                            preferred_element_type=F32)
            lo = jnp.clip(t - w // 2, 0, seq)
            hi = jnp.clip(t - w // 2 + w, 0, seq)
            means.append((parts[:, 0:LANES] + parts[:, LANES:2 * LANES]) / (hi - lo).astype(F32))
        pooled = jnp.concatenate(means, axis=1) - p_ref[rows, 0:D_BRANCH]
        out = jnp.dot(pooled.astype(BF16), wbd_s[...], preferred_element_type=F32) + b_ref[...]
        gate = p_ref[rows, D_BRANCH:2 * D_BRANCH]
        o_ref[rows, :] = (out * sc_ref[...] * _silu(gate)).astype(BF16)

    def blk_body(c2, carry):
        for u in range(POOL_UNROLL):
            one_chunk(c2 * POOL_UNROLL + u)
        return carry

    lax.fori_loop(0, ns * nb // POOL_UNROLL, blk_body, 0)


def _pool(proj, lp, *, n_seq, seq):
    ns = max(1, DEC_SEQ // seq)
    return pl.pallas_call(
        functools.partial(_pool_kernel, seq=seq, ns=ns),
        grid=(n_seq // ns,),
        in_specs=[
            pl.BlockSpec((ns * seq, 2 * D_BRANCH), lambda b: (b, 3)),
            pl.BlockSpec((4, LANES, LANES), lambda b: (0, 0, 0)),
            pl.BlockSpec((1, D_BRANCH), lambda b: (0, 0)),
            pl.BlockSpec((1, D_BRANCH), lambda b: (0, 0)),
        ],
        out_specs=pl.BlockSpec((ns * seq, D_BRANCH), lambda b: (b, 0)),
        out_shape=jax.ShapeDtypeStruct((n_seq * seq, D_BRANCH), BF16),
        scratch_shapes=[
            pltpu.VMEM((ns * (seq + 2 * POOL_HALO), D_BRANCH), BF16),
            pltpu.VMEM((ns * (seq + 2 * POOL_HALO), D_BRANCH), BF16),
            pltpu.VMEM((len(POOL_WINDOWS), CHUNK, POOL_WIN), BF16),
            pltpu.VMEM((D_BRANCH, D_BRANCH), BF16),
        ],
        compiler_params=pltpu.CompilerParams(vmem_limit_bytes=VMEM_LIMIT),
        name="pool",
    )(proj, lp["pool_w"], lp["pool_b"], lp["pool_scale"])


def _rope(x, cos, sin):
    lane = lax.broadcasted_iota(jnp.int32, x.shape, 1)
    first = (lane & 63) < 32
    swapped = jnp.where(first, pltpu.roll(x, 96, axis=1), pltpu.roll(x, 32, axis=1))
    return x * cos + swapped * sin


def _load_cache(c_ref, c_s, n_heads):
    for h in range(n_heads):
        c_s[:, h * HEAD_DIM:(h + 1) * HEAD_DIM] = c_ref[0, 0, pl.ds(h, PAST_LEN, stride=n_heads), :].astype(BF16)


def _win_attn_kernel(sink_ref, p_ref, kc_ref, vc_ref, cos_ref, sin_ref, o_ref, q_s, k_s, v_s, kc_s, vc_s, mask_s):
    seq = DEC_SEQ
    nb = seq // CHUNK
    kv_w = A_KV_HEADS * HEAD_DIM
    k_off = A_HEADS * HEAD_DIM
    v_off = k_off + kv_w
    g_off = v_off + kv_w
    zero_blk = jnp.zeros((CHUNK, kv_w), BF16)
    for s in (k_s, v_s):
        s[0:CHUNK, :] = zero_blk
        s[seq + CHUNK:seq + 2 * CHUNK, :] = zero_blk
    _load_cache(kc_ref, kc_s, A_KV_HEADS)
    _load_cache(vc_ref, vc_s, A_KV_HEADS)
    grp = A_HEADS // A_KV_HEADS

    @pl.when(pl.program_id(0) == 0)
    def _():
        ii = lax.broadcasted_iota(jnp.int32, (grp * CHUNK, 3 * CHUNK), 0)
        jj = lax.broadcasted_iota(jnp.int32, (grp * CHUNK, 3 * CHUNK), 1)
        rel = jj - CHUNK - (ii & (CHUNK - 1))
        band = (rel >= -A_WINDOW) & (rel <= A_WINDOW)
        mask_s[0] = jnp.where(band & (jj >= CHUNK), 0.0, NEG_INF)
        mask_s[1] = jnp.where(band, 0.0, NEG_INF)
        mask_s[2] = jnp.where(band & (jj < 2 * CHUNK), 0.0, NEG_INF)

    def prep_body(c, carry):
        r0 = pl.multiple_of(c * CHUNK, CHUNK)
        rows = pl.ds(r0, CHUNK)
        prow = pl.ds(pl.multiple_of(r0 + CHUNK, CHUNK), CHUNK)
        cos = cos_ref[rows, :]
        sin = sin_ref[rows, :]
        for h in range(A_HEADS):
            cols = slice(h * HEAD_DIM, (h + 1) * HEAD_DIM)
            q_s[rows, cols] = _scaled_query(_rope(p_ref[rows, cols], cos, sin))
        for kk in range(A_KV_HEADS):
            cols = slice(kk * HEAD_DIM, (kk + 1) * HEAD_DIM)
            k_s[prow, cols] = _rope(p_ref[rows, k_off + kk * HEAD_DIM:k_off + (kk + 1) * HEAD_DIM], cos, sin).astype(BF16)
        v_s[prow, :] = p_ref[rows, v_off:v_off + kv_w].astype(BF16)
        return carry

    lax.fori_loop(0, nb, prep_body, 0)

    def blk_body(n, carry):
        r0 = pl.multiple_of(n * CHUNK, CHUNK)
        rows = pl.ds(r0, CHUNK)
        win = pl.ds(r0, 3 * CHUNK)
        mask = mask_s[jnp.where(n == 0, 0, jnp.where(n == nb - 1, 2, 1))]
        first_head = lax.broadcasted_iota(jnp.int32, (grp * CHUNK, 1), 0) < CHUNK
        for kk in range(A_KV_HEADS):
            kcols = slice(kk * HEAD_DIM, (kk + 1) * HEAD_DIM)
            h0 = kk * grp
            q = jnp.concatenate([q_s[rows, (h0 + g) * HEAD_DIM:(h0 + g + 1) * HEAD_DIM] for g in range(grp)], axis=0)
            s_loc = lax.dot_general(q, k_s[win, kcols], TRANS_B, preferred_element_type=F32) + mask
            s_ctx = lax.dot_general(q, kc_s[:, kcols], TRANS_B, preferred_element_type=F32)
            sink = jnp.where(first_head, sink_ref[h0], sink_ref[h0 + 1])
            o = _softmax_av([s_loc, s_ctx], [v_s[win, kcols], vc_s[:, kcols]], sink)
            for g in range(grp):
                cols = slice((h0 + g) * HEAD_DIM, (h0 + g + 1) * HEAD_DIM)
                gate = p_ref[rows, g_off + (h0 + g) * HEAD_DIM:g_off + (h0 + g + 1) * HEAD_DIM]
                o_ref[rows, cols] = (o[g * CHUNK:(g + 1) * CHUNK, :] * _silu(gate)).astype(BF16)
        return carry

    lax.fori_loop(0, nb, blk_body, 0)


def _win_attn(proj, sink, kc, vc, cos_t, sin_t, layer):
    seq = DEC_SEQ
    kv_w = A_KV_HEADS * HEAD_DIM
    cache_spec = pl.BlockSpec((1, 1, PAST_LEN * A_KV_HEADS, HEAD_DIM), lambda b: (b, layer, 0, 0))
    tab_spec = pl.BlockSpec((seq, HEAD_DIM), lambda b: (0, 0))
    return pl.pallas_call(
        _win_attn_kernel,
        grid=(DEC_BATCH,),
        in_specs=[
            pl.BlockSpec(memory_space=pltpu.SMEM),
            pl.BlockSpec((seq, 3 * D_BRANCH), lambda b: (b, 0)),
            cache_spec, cache_spec, tab_spec, tab_spec,
        ],
        out_specs=pl.BlockSpec((seq, D_BRANCH), lambda b: (b, 0)),
        out_shape=jax.ShapeDtypeStruct((DEC_BATCH * seq, D_BRANCH), BF16),
        scratch_shapes=[
            pltpu.VMEM((seq, A_HEADS * HEAD_DIM), BF16),
            pltpu.VMEM((seq + 2 * CHUNK, kv_w), BF16),
            pltpu.VMEM((seq + 2 * CHUNK, kv_w), BF16),
            pltpu.VMEM((PAST_LEN, kv_w), BF16),
            pltpu.VMEM((PAST_LEN, kv_w), BF16),
            pltpu.VMEM((3, (A_HEADS // A_KV_HEADS) * CHUNK, 3 * CHUNK), F32),
        ],
        compiler_params=pltpu.CompilerParams(vmem_limit_bytes=VMEM_LIMIT),
        name="win_attn",
    )(sink, proj, kc, vc, cos_t, sin_t)


NA_ROWS = DEC_SEQ // GRID_W
NA_KROWS = min(NA_KH, NA_ROWS)
NA_PAIRS = 2 * NA_KH - 2
NA_QROWS = 4
NA_WROWS = NA_QROWS + NA_KROWS
NA_GROUPS = NA_ROWS // NA_QROWS


def _na_row_start(r):
    return min(max(r - NA_KROWS // 2, 0), NA_ROWS - NA_KROWS)


def _na_win_start(qb):
    return min(max(qb * NA_QROWS - NA_KROWS // 2, 0), NA_ROWS - NA_WROWS)


def _na_bias_kernel(rpb_ref, o_ref, pair_s):
    layer = pl.program_id(0)
    h = pl.program_id(1)
    n_dy = 2 * NA_KH - 1
    n_dx = 2 * NA_KW - 1
    base = (layer * D_HEADS + h) * (n_dy * n_dx)
    qc = lax.broadcasted_iota(jnp.int32, (GRID_W, 2 * GRID_W), 0)
    lane = lax.broadcasted_iota(jnp.int32, (GRID_W, 2 * GRID_W), 1)
    second = lane >= GRID_W
    kc = lane & (GRID_W - 1)
    idx = jnp.clip(kc - qc, -(NA_KW - 1), NA_KW - 1) + (NA_KW - 1)
    cs = jnp.clip(qc - NA_KW // 2, 0, GRID_W - NA_KW)
    col_ok = (kc >= cs) & (kc < cs + NA_KW)
    for e in range(NA_PAIRS):
        val = jnp.zeros((GRID_W, 2 * GRID_W), F32)
        for d in range(n_dx):
            r0 = rpb_ref[base + e * n_dx + d]
            r1 = rpb_ref[base + (e + 1) * n_dx + d]
            val = jnp.where(idx == d, jnp.where(second, r1, r0), val)
        pair_s[e] = jnp.where(col_ok, val * LOG2E, NEG_INF)

    masked = jnp.full((GRID_W, 2 * GRID_W), NEG_INF, F32)
    for qb in range(NA_GROUPS):
        ws = _na_win_start(qb)
        for ri in range(NA_QROWS):
            r = qb * NA_QROWS + ri
            rs = _na_row_start(r)
            assert ws <= rs and rs + NA_KROWS <= ws + NA_WROWS
            for p in range(NA_WROWS // 2):
                kr0 = ws + 2 * p
                ok0 = rs <= kr0 < rs + NA_KROWS
                ok1 = rs <= kr0 + 1 < rs + NA_KROWS
                e = kr0 - r + (NA_KH - 1)
                if ok0 or ok1:
                    assert 0 <= e < NA_PAIRS
                    blk = pair_s[e]
                    if not ok1:
                        blk = jnp.where(second, NEG_INF, blk)
                    if not ok0:
                        blk = jnp.where(second, blk, NEG_INF)
                else:
                    blk = masked
                o_ref[0, 0, qb, ri * GRID_W:(ri + 1) * GRID_W, p * 2 * GRID_W:(p + 1) * 2 * GRID_W] = blk


def _na_bias(na_rpb):
    blk = (NA_GROUPS, NA_QROWS * GRID_W, NA_WROWS * GRID_W)
    return pl.pallas_call(
        _na_bias_kernel,
        grid=(DEPTH, D_HEADS),
        in_specs=[pl.BlockSpec(memory_space=pltpu.SMEM)],
        out_specs=pl.BlockSpec((1, 1) + blk, lambda l, h: (l, h, 0, 0, 0)),
        out_shape=jax.ShapeDtypeStruct((DEPTH, D_HEADS) + blk, F32),
        scratch_shapes=[pltpu.VMEM((NA_PAIRS, GRID_W, 2 * GRID_W), F32)],
        name="na_bias",
    )(na_rpb.reshape(-1))


def _na_kernel(q_ref, k_ref, v_ref, g_ref, kc_ref, vc_ref, tab_ref, o_ref, k_s, v_s, kc_s, vc_s):
    h = pl.program_id(1)

    @pl.when(h == 0)
    def _():
        for hh in range(D_HEADS):
            kc_s[hh] = kc_ref[0, 0, pl.ds(hh, PAST_LEN, stride=D_HEADS), :].astype(BF16)
            vc_s[hh] = vc_ref[0, 0, pl.ds(hh, PAST_LEN, stride=D_HEADS), :].astype(BF16)

    k_s[...] = k_ref[...].astype(BF16)
    v_s[...] = v_ref[...].astype(BF16)
    kc = kc_s[h]
    vc = vc_s[h]
    nq = NA_QROWS * GRID_W
    for qb in range(NA_GROUPS):
        rows = slice(qb * nq, (qb + 1) * nq)
        ws = _na_win_start(qb)
        krows = slice(ws * GRID_W, (ws + NA_WROWS) * GRID_W)
        q = _scaled_query(q_ref[rows, :])
        s_loc = lax.dot_general(q, k_s[krows, :], TRANS_B, preferred_element_type=F32) + tab_ref[0, 0, qb]
        s_ctx = lax.dot_general(q, kc, TRANS_B, preferred_element_type=F32)
        o = _softmax_av([s_loc, s_ctx], [v_s[krows, :], vc])
        o_ref[rows, :] = (o * _silu(g_ref[rows, :])).astype(BF16)


def _na_attn(proj, kc, vc, tab, layer):
    seq = DEC_SEQ
    first = (D_MAIN - 4 * D_HEADS * HEAD_DIM) // HEAD_DIM
    col_spec = lambda g: pl.BlockSpec((seq, HEAD_DIM), lambda b, h: (b, first + g * D_HEADS + h))
    cache_spec = pl.BlockSpec((1, 1, PAST_LEN * D_HEADS, HEAD_DIM), lambda b, h: (b, layer, 0, 0))
    tab_blk = (NA_GROUPS, NA_QROWS * GRID_W, NA_WROWS * GRID_W)
    return pl.pallas_call(
        _na_kernel,
        grid=(DEC_BATCH, D_HEADS),
        in_specs=[
            col_spec(0), col_spec(1), col_spec(2), col_spec(3),
            cache_spec, cache_spec,
            pl.BlockSpec((1, 1) + tab_blk, lambda b, h: (layer, h, 0, 0, 0)),
        ],
        out_specs=pl.BlockSpec((seq, HEAD_DIM), lambda b, h: (b, h)),
        out_shape=jax.ShapeDtypeStruct((DEC_BATCH * seq, D_HEADS * HEAD_DIM), BF16),
        scratch_shapes=[pltpu.VMEM((seq, HEAD_DIM), BF16)] * 2 + [pltpu.VMEM((D_HEADS, PAST_LEN, HEAD_DIM), BF16)] * 2,
        compiler_params=pltpu.CompilerParams(
            dimension_semantics=("arbitrary", "arbitrary"), vmem_limit_bytes=VMEM_LIMIT),
        name="na_attn",
    )(proj, proj, proj, proj, kc, vc, tab)


def _rope_tables(seq):
    t = jnp.arange(seq)
    rows = (t // GRID_W).astype(F32)
    cols = (t % GRID_W).astype(F32)
    nf = HEAD_DIM // 4
    inv = ROPE_THETA ** (-jnp.arange(nf, dtype=F32) / nf)
    ar = rows[:, None] * inv[None, :]
    ac = cols[:, None] * inv[None, :]
    cos_t = jnp.concatenate([jnp.cos(ar), jnp.cos(ar), jnp.cos(ac), jnp.cos(ac)], axis=-1)
    sin_t = jnp.concatenate([-jnp.sin(ar), jnp.sin(ar), -jnp.sin(ac), jnp.sin(ac)], axis=-1)
    return cos_t, sin_t


def _pad_lanes(v, width=LANES):
    v = v.reshape(1, -1)
    return jnp.pad(v, ((0, 0), (0, width - v.shape[1])))


def kernel(x_prompt, x_sample, cache_attn_k, cache_attn_v, cache_na_k, cache_na_v, state_ssm_fwd, state_ssm_bwd, c, c_ctx, w_ada, b_ada, w_in, w_out, ln_g, ln_b, attn_sink, ssm_conv_w, ssm_conv_b, ssm_a_log, ssm_dt_bias, ssm_d, ssm_norm_w, pool_w, pool_b, pool_scale, na_rpb):
    alpha = (2.0 * DEPTH) ** 0.25
    n_x = B_HEADS * B_HEAD_DIM
    xc = x_prompt.reshape(BATCH * SEQ, D_MODEL)
    xl = x_sample.reshape(DEC_BATCH * DEC_SEQ, D_MODEL)

    cv = jnp.zeros((16, D_MODEL), F32).at[0].set(c_ctx).at[1:1 + DEC_BATCH].set(c)
    mod = _ada(cv, w_ada, b_ada)
    cos_t, sin_t = _rope_tables(DEC_SEQ)
    na_tab = _na_bias(na_rpb)

    kc_a = cache_attn_k.reshape(DEC_BATCH, DEPTH, PAST_LEN * A_KV_HEADS, HEAD_DIM)
    vc_a = cache_attn_v.reshape(DEC_BATCH, DEPTH, PAST_LEN * A_KV_HEADS, HEAD_DIM)
    kc_d = cache_na_k.reshape(DEC_BATCH, DEPTH, PAST_LEN * D_HEADS, HEAD_DIM)
    vc_d = cache_na_v.reshape(DEC_BATCH, DEPTH, PAST_LEN * D_HEADS, HEAD_DIM)
    h0 = (state_ssm_fwd.reshape(DEC_BATCH, DEPTH, n_x, B_STATE), state_ssm_bwd.reshape(DEC_BATCH, DEPTH, n_x, B_STATE))

    ctx_row = lambda i: 0
    lat_in_row = lambda i: 1 + i * IN_TM // DEC_SEQ
    lat_out_row = lambda i: 1 + i * OUT_TM // DEC_SEQ

    w_main, w_dt = _prep_w(w_in)
    w_o = w_out.astype(BF16)
    kv_a = kv_d = ssm_state = None
    for l in range(DEPTH):
        mod3 = mod[l].reshape(16, 1, 3 * D_MODEL)
        lp = {
            "conv_w": jnp.pad(ssm_conv_w[l], ((0, ROW_PAD - B_CONV), (0, 0))),
            "conv_b": ssm_conv_b[l].reshape(1, -1),
            "dt_bias": _pad_lanes(ssm_dt_bias[l]),
            "a_log": _pad_lanes(ssm_a_log[l]),
            "d_skip": jnp.repeat(ssm_d[l], B_HEAD_DIM).reshape(1, n_x),
            "norm_w": ssm_norm_w[l].reshape(1, n_x),
            "pool_w": pool_w[l],
            "pool_b": pool_b[l].reshape(1, D_BRANCH),
            "pool_scale": pool_scale[l].reshape(1, D_BRANCH),
        }
        sink = attn_sink[l]

        proj_c, dt_c = _inproj(xc, mod3, w_main, w_dt, ctx_row, l)
        o_a, *kv_a = _ctx_attn(proj_c, sink, kv_a, l, n_q=A_HEADS, n_kv=A_KV_HEADS, col_block=0,
                               width=3 * D_BRANCH, use_sink=True)
        o_b, *ssm_state = _ssm(proj_c, dt_c, lp, l, n_seq=BATCH, seq=SEQ, prev_state=ssm_state)
        o_c = _pool(proj_c, lp, n_seq=BATCH, seq=SEQ)
        o_d, *kv_d = _ctx_attn(proj_c, sink, kv_d, l, n_q=D_HEADS, n_kv=D_HEADS, col_block=2,
                               width=4 * D_BRANCH, use_sink=False)
        xc = _outproj((o_a, o_b, o_c, o_d), xc, mod3, w_o, ln_g[l], ln_b[l], ctx_row, alpha, l)

        proj_l, dt_l = _inproj(xl, mod3, w_main, w_dt, lat_in_row, l)
        o_a = _win_attn(proj_l, sink, kc_a, vc_a, cos_t, sin_t, l)
        o_b = _ssm(proj_l, dt_l, lp, l, n_seq=DEC_BATCH, seq=DEC_SEQ, h0=h0)[0]
        o_c = _pool(proj_l, lp, n_seq=DEC_BATCH, seq=DEC_SEQ)
        o_d = _na_attn(proj_l, kc_d, vc_d, na_tab, l)
        xl = _outproj((o_a, o_b, o_c, o_d), xl, mod3, w_o, ln_g[l], ln_b[l], lat_out_row, alpha, l)

    return (
        xc.reshape(BATCH, SEQ, D_MODEL),
        xl.reshape(DEC_BATCH, DEC_SEQ, D_MODEL),
        kv_a[0].reshape(BATCH, DEPTH, SEQ, A_KV_HEADS, HEAD_DIM),
        kv_a[1].reshape(BATCH, DEPTH, SEQ, A_KV_HEADS, HEAD_DIM),
        kv_d[0].reshape(BATCH, DEPTH, SEQ, D_HEADS, HEAD_DIM),
        kv_d[1].reshape(BATCH, DEPTH, SEQ, D_HEADS, HEAD_DIM),
        ssm_state[0].reshape(BATCH, DEPTH, B_HEADS, B_HEAD_DIM, B_STATE),
        ssm_state[1].reshape(BATCH, DEPTH, B_HEADS, B_HEAD_DIM, B_STATE),
    )
```
